```python
import math
import jax, jax.numpy as jnp
from jax import lax
import numpy as np

D_MODEL = 1024
BATCH = 4
SEQ = 4096
DEPTH = 1

N_ATTN_HEADS = 4
ATTN_HEAD_DIM = 64
ATTN_V_DIM = 2 * ATTN_HEAD_DIM
ATTN_QK_WIDTH = N_ATTN_HEADS * 2 * ATTN_HEAD_DIM
ATTN_WIDTH = N_ATTN_HEADS * ATTN_V_DIM
ROPE_THETA = 10000.0
Q_BLOCK = 128
CONV_WIDTH = 512
CONV_SIZE = 3
PEER_HEADS = 8
PEER_N_KEYS = 128
PEER_N_EXPERTS = PEER_N_KEYS * PEER_N_KEYS
PEER_HALF_DIM = 128
PEER_TOPK = 16
TOKEN_CHUNK = 128
NORM_EPS = 1e-6
IN_SIZES = (ATTN_QK_WIDTH, ATTN_QK_WIDTH, ATTN_WIDTH, CONV_WIDTH, CONV_WIDTH, CONV_WIDTH, D_MODEL, D_MODEL)
IN_WIDTH = 2 * ATTN_QK_WIDTH + ATTN_WIDTH + 3 * CONV_WIDTH + 2 * D_MODEL

kernel_name = "hybrid_diffattn_shortconv_peer_block"


def rmsnorm(x, w):
    xf = x.astype(jnp.float32)
    y = xf * lax.rsqrt(jnp.mean(xf * xf, axis=-1, keepdims=True) + NORM_EPS)
    return (y * w.astype(jnp.float32)).astype(x.dtype)


def rope_tables(seq, dim):
    inv_freq = 1.0 / (ROPE_THETA ** (jnp.arange(0, dim, 2, dtype=jnp.float32) / dim))
    ang = jnp.arange(seq, dtype=jnp.float32)[:, None] * inv_freq[None, :]
    ang = jnp.concatenate([ang, ang], axis=-1)
    return jnp.cos(ang), jnp.sin(ang)


def apply_rope(t, cos, sin):
    half = t.shape[-1] // 2
    t1, t2 = t[..., :half], t[..., half:]
    rot = jnp.concatenate([-t2, t1], axis=-1)
    c = cos[None, :, None, None, :].astype(t.dtype)
    s = sin[None, :, None, None, :].astype(t.dtype)
    return t * c + rot * s


def diff_attention(q, k, v, lam):
    b, s, h, _, dh = q.shape
    scale = 1.0 / math.sqrt(dh)
    nb = s // Q_BLOCK
    qb = jnp.moveaxis(q.reshape(b, nb, Q_BLOCK, h, 2, dh), 1, 0)

    def block(qblk):
        sc = jnp.einsum('bqhmd,bkhmd->bhmqk', qblk, k).astype(jnp.float32) * scale
        p = jax.nn.softmax(sc, axis=-1)
        diff = p[:, :, 0] - lam * p[:, :, 1]
        return jnp.einsum('bhqk,bkhe->bqhe', diff.astype(v.dtype), v)

    out = lax.map(block, qb)
    return jnp.moveaxis(out, 0, 1).reshape(b, s, h, v.shape[-1])


def short_conv(z, conv_w):
    c = z.shape[-1]
    return lax.conv_general_dilated(
        z, conv_w.reshape(CONV_SIZE, 1, c).astype(z.dtype), window_strides=(1,),
        padding=((CONV_SIZE // 2, CONV_SIZE // 2),),
        dimension_numbers=('NWC', 'WIO', 'NWC'), feature_group_count=c)


def mixer_block(xn, w_in, lq1, lk1, lq2, lk2, subln_w, conv_w, w_pa, w_pb, w_o, cos, sin, lambda_init):
    b, s, _ = xn.shape
    proj = xn @ w_in
    offs, acc = [], 0
    for sz in IN_SIZES[:-1]:
        acc += sz
        offs.append(acc)
    q, k, v, bg, cg, xc, ga, gb = jnp.split(proj, offs, axis=-1)

    q = apply_rope(q.reshape(b, s, N_ATTN_HEADS, 2, ATTN_HEAD_DIM), cos, sin)
    k = apply_rope(k.reshape(b, s, N_ATTN_HEADS, 2, ATTN_HEAD_DIM), cos, sin)
    v = v.reshape(b, s, N_ATTN_HEADS, ATTN_V_DIM)
    lam = (jnp.exp(jnp.sum(lq1.astype(jnp.float32) * lk1.astype(jnp.float32)))
           - jnp.exp(jnp.sum(lq2.astype(jnp.float32) * lk2.astype(jnp.float32))) + lambda_init)
    attn = diff_attention(q, k, v, lam)
    attn = rmsnorm(attn, subln_w) * (1.0 - lambda_init)
    y_attn = attn.reshape(b, s, ATTN_WIDTH) @ w_pa

    y_conv = (bg * short_conv(cg * xc, conv_w)) @ w_pb

    merged = jax.nn.sigmoid(ga) * y_attn + jax.nn.sigmoid(gb) * y_conv
    return merged @ w_o


def peer_ffn(hn, w_query, sub_keys, expert_u, expert_v):
    b, s, d = hn.shape
    chunks = hn.reshape(-1, TOKEN_CHUNK, d)

    def chunk(xc):
        c = xc.shape[0]
        q = (xc @ w_query).reshape(c, PEER_HEADS, 2, PEER_HALF_DIM)
        sc = jnp.einsum('chpd,hpnd->chpn', q, sub_keys).astype(jnp.float32)
        s1, i1 = lax.top_k(sc[:, :, 0], PEER_TOPK)
        s2, i2 = lax.top_k(sc[:, :, 1], PEER_TOPK)
        cand = (s1[..., :, None] + s2[..., None, :]).reshape(c, PEER_HEADS, PEER_TOPK * PEER_TOPK)
        cidx = (i1[..., :, None] * PEER_N_KEYS + i2[..., None, :]).reshape(c, PEER_HEADS, PEER_TOPK * PEER_TOPK)
        top, pos = lax.top_k(cand, PEER_TOPK)
        eidx = jnp.take_along_axis(cidx, pos, axis=-1)
        g = jax.nn.softmax(top, axis=-1)
        u = expert_u[eidx]
        a = jax.nn.gelu(jnp.einsum('chkd,cd->chk', u, xc), approximate=False)
        vv = expert_v[eidx]
        return jnp.einsum('chk,chkd->cd', (g.astype(a.dtype) * a), vv)

    out = lax.map(chunk, chunks)
    return out.reshape(b, s, d)


def setup_inputs(seed: int = 0) -> dict:
    key = jax.random.key(seed)
    ks = jax.random.split(key, 20)
    f32 = jnp.float32
    n = lambda k, shape, sc: jax.random.normal(k, shape, f32) * sc
    return {
        "x": n(ks[0], (BATCH, SEQ, D_MODEL), 1.0),
        "attn_norm_w": 1.0 + n(ks[1], (DEPTH, D_MODEL), 0.01),
        "w_in": n(ks[2], (DEPTH, D_MODEL, IN_WIDTH), D_MODEL ** -0.5),
        "lambda_q1": n(ks[3], (DEPTH, ATTN_HEAD_DIM), 0.1),
        "lambda_k1": n(ks[4], (DEPTH, ATTN_HEAD_DIM), 0.1),
        "lambda_q2": n(ks[5], (DEPTH, ATTN_HEAD_DIM), 0.1),
        "lambda_k2": n(ks[6], (DEPTH, ATTN_HEAD_DIM), 0.1),
        "subln_w": 1.0 + n(ks[7], (DEPTH, ATTN_V_DIM), 0.01),
        "conv_w": n(ks[8], (DEPTH, CONV_SIZE, CONV_WIDTH), CONV_SIZE ** -0.5),
        "w_proj_attn": n(ks[9], (DEPTH, ATTN_WIDTH, D_MODEL), ATTN_WIDTH ** -0.5),
        "w_proj_conv": n(ks[10], (DEPTH, CONV_WIDTH, D_MODEL), CONV_WIDTH ** -0.5),
        "w_out": n(ks[11], (DEPTH, D_MODEL, D_MODEL), D_MODEL ** -0.5),
        "ffn_norm_w": 1.0 + n(ks[12], (DEPTH, D_MODEL), 0.01),
        "w_query": n(ks[13], (DEPTH, D_MODEL, PEER_HEADS * 2 * PEER_HALF_DIM), D_MODEL ** -0.5),
        "sub_keys": n(ks[14], (DEPTH, PEER_HEADS, 2, PEER_N_KEYS, PEER_HALF_DIM), PEER_HALF_DIM ** -0.5),
        "expert_u": n(ks[15], (DEPTH, PEER_N_EXPERTS, D_MODEL), D_MODEL ** -0.5),
        "expert_v": n(ks[16], (DEPTH, PEER_N_EXPERTS, D_MODEL), PEER_HEADS ** -0.5),
        "final_norm_w": 1.0 + n(ks[17], (D_MODEL,), 0.01),
    }


def reference(x, attn_norm_w, w_in, lambda_q1, lambda_k1, lambda_q2, lambda_k2, subln_w, conv_w,
              w_proj_attn, w_proj_conv, w_out, ffn_norm_w, w_query, sub_keys, expert_u, expert_v,
              final_norm_w):
    cos, sin = rope_tables(x.shape[1], ATTN_HEAD_DIM)
    h = x
    for l in range(DEPTH):
        lambda_init = 0.8 - 0.6 * math.exp(-0.3 * l)
        xn = rmsnorm(h, attn_norm_w[l])
        h = h + mixer_block(xn, w_in[l], lambda_q1[l], lambda_k1[l], lambda_q2[l], lambda_k2[l],
                            subln_w[l], conv_w[l], w_proj_attn[l], w_proj_conv[l], w_out[l],
                            cos, sin, lambda_init)
        hn = rmsnorm(h, ffn_norm_w[l])
        h = h + peer_ffn(hn, w_query[l], sub_keys[l], expert_u[l], expert_v[l])
    return rmsnorm(h, final_norm_w)
```

```python
import functools
import math

import jax
import jax.numpy as jnp
from jax import lax
from jax.experimental import pallas as pl
from jax.experimental.pallas import tpu as pltpu

NORM_EPS = 1e-6
ROPE_THETA = 10000.0
PEER_TOPK = 16
LANES = 128
SUBLANES = 8
VMEM_LIMIT_BYTES = 56 * 1024 * 1024

F32 = jnp.float32
BF16 = jnp.bfloat16
NEG_INF = float("-inf")


def _nt_dot(a, b):
    return lax.dot_general(a, b, (((1,), (1,)), ((), ())), preferred_element_type=F32)


def _dot(a, b):
    return jnp.dot(a, b, preferred_element_type=F32)


def _const_spec(shape):
    nd = len(shape)
    return pl.BlockSpec(shape, lambda *_: (0,) * nd)


def _inproj_kernel(x_ref, nw_ref, cos_ref, sin_ref, wq_ref, wqr_ref, wk_ref, wkr_ref, wv_ref,
                   wb_ref, wc_ref, wx_ref, wga_ref, wgb_ref,
                   q_ref, k_ref, v_ref, bg_ref, u_ref, sga_ref, sgb_ref, *, q_scale):
    x = x_ref[...]
    ms = jnp.mean(x * x, axis=-1, keepdims=True)
    xn = (x * lax.rsqrt(ms + NORM_EPS) * nw_ref[...]).astype(BF16)
    cos = cos_ref[...]
    sin = sin_ref[...]

    def rope(w_ref, wr_ref, o_ref, scale):
        t = _dot(xn, w_ref[...])
        tr = _dot(xn, wr_ref[...])
        for blk in range(t.shape[1] // LANES):
            sl = slice(blk * LANES, (blk + 1) * LANES)
            r = t[:, sl] * cos + tr[:, sl] * sin
            if scale != 1.0:
                r = r * scale
            o_ref[:, sl] = r.astype(o_ref.dtype)

    rope(wq_ref, wqr_ref, q_ref, q_scale)
    rope(wk_ref, wkr_ref, k_ref, 1.0)
    v_ref[...] = _dot(xn, wv_ref[...]).astype(v_ref.dtype)
    bg_ref[...] = _dot(xn, wb_ref[...])
    u_ref[...] = _dot(xn, wc_ref[...]) * _dot(xn, wx_ref[...])
    sga_ref[...] = jax.nn.sigmoid(_dot(xn, wga_ref[...]))
    sgb_ref[...] = jax.nn.sigmoid(_dot(xn, wgb_ref[...]))


def _inproj(xf, norm_w, cos, sin, ws, *, seq, tm, q_scale):
    t, d = xf.shape
    wq, wqr, wk, wkr, wv, wb, wc, wx, wga, wgb = ws
    n_seq_tiles = seq // tm
    row = lambda c: pl.BlockSpec((tm, c), lambda i: (i, 0))
    out_shapes = (
        jax.ShapeDtypeStruct((t, wq.shape[1]), BF16),
        jax.ShapeDtypeStruct((t, wk.shape[1]), BF16),
        jax.ShapeDtypeStruct((t, wv.shape[1]), BF16),
        jax.ShapeDtypeStruct((t, wb.shape[1]), F32),
        jax.ShapeDtypeStruct((t, wc.shape[1]), F32),
        jax.ShapeDtypeStruct((t, wga.shape[1]), F32),
        jax.ShapeDtypeStruct((t, wgb.shape[1]), F32),
    )
    return pl.pallas_call(
        functools.partial(_inproj_kernel, q_scale=q_scale),
        grid=(t // tm,),
        in_specs=[row(d), _const_spec((1, d)),
                  pl.BlockSpec((tm, LANES), lambda i: (i % n_seq_tiles, 0)),
                  pl.BlockSpec((tm, LANES), lambda i: (i % n_seq_tiles, 0))]
                 + [_const_spec(w.shape) for w in ws],
        out_specs=[row(s.shape[1]) for s in out_shapes],
        out_shape=out_shapes,
        compiler_params=pltpu.CompilerParams(dimension_semantics=("arbitrary",),
                                             vmem_limit_bytes=VMEM_LIMIT_BYTES),
        name="inproj",
    )(xf, norm_w, cos, sin, *ws)


def _attn_kernel(lq1_ref, lk1_ref, lq2_ref, lk2_ref, sw_ref, q_ref, k_ref, v_ref, o_ref,
                 *, tk, head_dim, lambda_init):
    tq = q_ref.shape[0]
    seq = k_ref.shape[0]
    q = q_ref[...]
    lane = lax.broadcasted_iota(jnp.int32, q.shape, 1)
    zero = jnp.zeros_like(q)
    q1 = jnp.where(lane < head_dim, q, zero)
    q2 = jnp.where(lane >= head_dim, q, zero)
    lam = (jnp.exp(jnp.sum(lq1_ref[...] * lk1_ref[...], axis=-1, keepdims=True))
           - jnp.exp(jnp.sum(lq2_ref[...] * lk2_ref[...], axis=-1, keepdims=True)) + lambda_init)

    def update(qm, kc, vc, m, l, a):
        s = _nt_dot(qm, kc)
        mn = jnp.maximum(m, jnp.max(s, axis=-1, keepdims=True))
        p = jnp.exp(s - mn)
        al = jnp.exp(m - mn)
        l = al * l + jnp.sum(p, axis=-1, keepdims=True)
        a = al * a + _dot(p.astype(BF16), vc)
        return mn, l, a

    def body(j, carry):
        m1, l1, a1, m2, l2, a2 = carry
        start = pl.multiple_of(j * tk, tk)
        kc = k_ref[pl.ds(start, tk), :]
        vc = v_ref[pl.ds(start, tk), :]
        m1, l1, a1 = update(q1, kc, vc, m1, l1, a1)
        m2, l2, a2 = update(q2, kc, vc, m2, l2, a2)
        return m1, l1, a1, m2, l2, a2

    ev = v_ref.shape[1]
    init = (jnp.full((tq, 1), NEG_INF, F32), jnp.zeros((tq, 1), F32), jnp.zeros((tq, ev), F32),
            jnp.full((tq, 1), NEG_INF, F32), jnp.zeros((tq, 1), F32), jnp.zeros((tq, ev), F32))
    m1, l1, a1, m2, l2, a2 = lax.fori_loop(0, seq // tk, body, init)
    o = a1 / l1 - lam * (a2 / l2)
    ms = jnp.mean(o * o, axis=-1, keepdims=True)
    o = o * lax.rsqrt(ms + NORM_EPS) * sw_ref[...]
    o_ref[...] = (o * (1.0 - lambda_init)).astype(o_ref.dtype)


def _attention(q, k, v, lq1, lk1, lq2, lk2, subln_w, *, batch, seq, n_heads, tq, tk, lambda_init):
    t = q.shape[0]
    ev = v.shape[1] // n_heads
    head_dim = lq1.shape[-1]
    nq = seq // tq
    small = lambda a: _const_spec(a.shape)
    return pl.pallas_call(
        functools.partial(_attn_kernel, tk=tk, head_dim=head_dim, lambda_init=lambda_init),
        grid=(batch, n_heads, nq),
        in_specs=[small(lq1), small(lk1), small(lq2), small(lk2), small(subln_w),
                  pl.BlockSpec((tq, 2 * head_dim), lambda b, h, i: (b * nq + i, h)),
                  pl.BlockSpec((seq, 2 * head_dim), lambda b, h, i: (b, h)),
                  pl.BlockSpec((seq, ev), lambda b, h, i: (b, h))],
        out_specs=pl.BlockSpec((tq, ev), lambda b, h, i: (b * nq + i, h)),
        out_shape=jax.ShapeDtypeStruct((t, v.shape[1]), BF16),
        compiler_params=pltpu.CompilerParams(
            dimension_semantics=("arbitrary", "arbitrary", "arbitrary"),
            vmem_limit_bytes=VMEM_LIMIT_BYTES),
        name="diff_attn",
    )(lq1, lk1, lq2, lk2, subln_w, q, k, v)


def _merge_kernel(attn_ref, bg_ref, u_ref, up_ref, un_ref, sga_ref, sgb_ref, x_ref,
                  cw_ref, wpa_ref, wpb_ref, wo_ref, fw_ref, wqy_ref, sk_ref,
                  h_ref, hn_ref, sc_ref, *, n_seq_tiles):
    i = pl.program_id(0)
    tm = u_ref.shape[0]
    u = u_ref[...]
    row = lax.broadcasted_iota(jnp.int32, u.shape, 0)
    first = (i % n_seq_tiles) == 0
    last = (i % n_seq_tiles) == n_seq_tiles - 1
    prev_row = jnp.where(first, 0.0, up_ref[SUBLANES - 1:SUBLANES, :])
    next_row = jnp.where(last, 0.0, un_ref[0:1, :])
    u_prev = jnp.where(row == 0, prev_row, pltpu.roll(u, 1, axis=0))
    u_next = jnp.where(row == tm - 1, next_row, pltpu.roll(u, tm - 1, axis=0))
    cw = cw_ref[...]
    conv = cw[0:1, :] * u_prev + cw[1:2, :] * u + cw[2:3, :] * u_next
    y_conv = _dot((bg_ref[...] * conv).astype(BF16), wpb_ref[...])
    y_attn = _dot(attn_ref[...], wpa_ref[...])
    merged = sga_ref[...] * y_attn + sgb_ref[...] * y_conv
    h = x_ref[...] + _dot(merged.astype(BF16), wo_ref[...])
    h_ref[...] = h
    ms = jnp.mean(h * h, axis=-1, keepdims=True)
    hn = (h * lax.rsqrt(ms + NORM_EPS) * fw_ref[...]).astype(BF16)
    hn_ref[...] = hn
    qp = _dot(hn, wqy_ref[...]).astype(BF16)
    for g in range(sk_ref.shape[0]):
        sc_ref[g] = _nt_dot(sk_ref[g], qp[:, g * LANES:(g + 1) * LANES])


def _merge(attn, bg, u, sga, sgb, xf, conv_w, wpa, wpb, wo, ffn_w, wqy, sk, *, seq, tm):
    t, d = xf.shape
    cw = u.shape[1]
    n_tiles = t // tm
    n_seq_tiles = seq // tm
    rb = tm // SUBLANES
    n_rb = t // SUBLANES
    row = lambda c: pl.BlockSpec((tm, c), lambda i: (i, 0))
    n_groups, n_keys, _ = sk.shape
    out_shapes = (jax.ShapeDtypeStruct((t, d), F32),
                  jax.ShapeDtypeStruct((t, d), BF16),
                  jax.ShapeDtypeStruct((n_groups, n_keys, t), F32))
    return pl.pallas_call(
        functools.partial(_merge_kernel, n_seq_tiles=n_seq_tiles),
        grid=(n_tiles,),
        in_specs=[row(attn.shape[1]), row(cw), row(cw),
                  pl.BlockSpec((SUBLANES, cw), lambda i: (jnp.maximum(i * rb - 1, 0), 0)),
                  pl.BlockSpec((SUBLANES, cw), lambda i: (jnp.minimum((i + 1) * rb, n_rb - 1), 0)),
                  row(d), row(d), row(d),
                  _const_spec(conv_w.shape), _const_spec(wpa.shape), _const_spec(wpb.shape),
                  _const_spec(wo.shape), _const_spec(ffn_w.shape), _const_spec(wqy.shape),
                  _const_spec(sk.shape)],
        out_specs=[row(d), row(d), pl.BlockSpec((n_groups, n_keys, tm), lambda i: (0, 0, i))],
        out_shape=out_shapes,
        compiler_params=pltpu.CompilerParams(dimension_semantics=("arbitrary",),
                                             vmem_limit_bytes=VMEM_LIMIT_BYTES),
        name="merge",
    )(attn, bg, u, u, u, sga, sgb, xf, conv_w, wpa, wpb, wo, ffn_w, wqy, sk)


def _top16(s):
    n = s.shape[0]
    idx = lax.broadcasted_iota(jnp.int32, s.shape, 0)
    rank = jnp.full(s.shape, PEER_TOPK, jnp.int32)
    cur = s
    vals = []
    for r in range(PEER_TOPK):
        mx = jnp.max(cur, axis=0, keepdims=True)
        first = jnp.min(jnp.where(cur == mx, idx, n), axis=0, keepdims=True)
        hit = idx == first
        rank = jnp.where(hit, r, rank)
        cur = jnp.where(hit, NEG_INF, cur)
        vals.append(mx)
    return rank, vals


def _route_kernel(sc_ref, c1_ref, r2_ref, a_ref, b_ref):
    tl = sc_ref.shape[-1]
    k = PEER_TOPK
    i16 = lax.broadcasted_iota(jnp.int32, (k, tl), 0)
    i8 = lax.broadcasted_iota(jnp.int32, (SUBLANES, tl), 0)
    order = [i16] + [r1 * k + i8 for r1 in range(1, 8)] + [(8 + i8) * k]
    order = jnp.concatenate(order, axis=0)
    valid = [i16 >= 0] + [i8 < (k // (r1 + 1)) for r1 in range(1, 8)] + [i8 >= 0]
    valid = jnp.concatenate(valid, axis=0)
    big = 2 * k * k

    def head(h, carry):
        s1 = sc_ref[2 * h]
        s2 = sc_ref[2 * h + 1]
        rank1, v1 = _top16(s1)
        rank2, v2 = _top16(s2)
        v2a = jnp.concatenate(v2, axis=0)
        v1b = jnp.concatenate(v1[8:], axis=0)
        e1 = [jnp.exp(v - v1[0]) for v in v1]
        e2a = jnp.exp(v2a - v2[0])
        e1b = jnp.concatenate(e1[8:], axis=0)
        cand = ([v1[0] + v2a] + [v1[r1] + v2a[:SUBLANES] for r1 in range(1, 8)]
                + [v1b + v2[0]])
        cand = jnp.where(valid, jnp.concatenate(cand, axis=0), NEG_INF)
        prod = ([e1[0] * e2a] + [e1[r1] * e2a[:SUBLANES] for r1 in range(1, 8)]
                + [e1b * e2a[0:1]])
        prod = jnp.concatenate(prod, axis=0)
        taken = jnp.zeros(cand.shape, F32)
        cur = cand
        for _ in range(k):
            mx = jnp.max(cur, axis=0, keepdims=True)
            first = jnp.min(jnp.where(cur == mx, order, big), axis=0, keepdims=True)
            hit = order == first
            taken = jnp.where(hit, 1.0, taken)
            cur = jnp.where(hit, NEG_INF, cur)
        z = jnp.sum(taken * prod, axis=0, keepdims=True)
        counts = [jnp.sum(taken[0:k], axis=0, keepdims=True)]
        for r1 in range(1, 8):
            lo = k + (r1 - 1) * SUBLANES
            counts.append(jnp.sum(taken[lo:lo + SUBLANES], axis=0, keepdims=True))
        lo = k + 7 * SUBLANES
        counts += [taken[lo + j:lo + j + 1] for j in range(8)]
        c1 = jnp.zeros(s1.shape, F32)
        for r in range(k):
            c1 = jnp.where(rank1 == r, counts[r], c1)
        c1_ref[h] = c1
        r2_ref[h] = rank2.astype(F32)
        a_ref[h] = jnp.exp(s1 - v1[0]) / z
        b_ref[h] = jnp.exp(s2 - v2[0])
        return carry

    lax.fori_loop(0, sc_ref.shape[0] // 2, head, 0)


def _route(sc, *, tl):
    n_groups, n_keys, t = sc.shape
    nh = n_groups // 2
    out = jax.ShapeDtypeStruct((nh, n_keys, t), F32)
    spec = pl.BlockSpec((nh, n_keys, tl), lambda i: (0, 0, i))
    return pl.pallas_call(
        _route_kernel,
        grid=(t // tl,),
        in_specs=[pl.BlockSpec((n_groups, n_keys, tl), lambda i: (0, 0, i))],
        out_specs=[spec] * 4,
        out_shape=(out,) * 4,
        compiler_params=pltpu.CompilerParams(dimension_semantics=("arbitrary",),
                                             vmem_limit_bytes=VMEM_LIMIT_BYTES),
        name="peer_route",
    )(sc)


def _expert_kernel(hn_ref, u_ref, vt_ref, c1_ref, a_ref, r2_ref, b_ref, h_ref, fw_ref,
                   o_ref, acc_ref, wt_ref, *, final_norm):
    j = pl.program_id(1)
    n_heads, n_keys, tq = r2_ref.shape
    rows = c1_ref.shape[1]

    @pl.when(j == 0)
    def _():
        acc_ref[...] = jnp.zeros_like(acc_ref)

    ht = _nt_dot(u_ref[...], hn_ref[...])
    for r in range(rows):
        sl = slice(r * n_keys, (r + 1) * n_keys)
        x = ht[sl]
        act = 0.5 * x * (1.0 + lax.erf(x * (1.0 / math.sqrt(2.0))))
        g = jnp.zeros((n_keys, tq), F32)
        for hd in range(n_heads):
            sel = r2_ref[hd] < c1_ref[hd, r:r + 1, :]
            g = g + jnp.where(sel, b_ref[hd], 0.0) * a_ref[hd, r:r + 1, :]
        wt_ref[sl, :] = (g * act).astype(BF16)
    acc_ref[...] += _dot(vt_ref[...], wt_ref[...])

    @pl.when(j == pl.num_programs(1) - 1)
    def _():
        h = h_ref[...] + acc_ref[...].T
        if final_norm:
            ms = jnp.mean(h * h, axis=-1, keepdims=True)
            h = h * lax.rsqrt(ms + NORM_EPS) * fw_ref[...]
        o_ref[...] = h


def _experts(hn, u_bf, vt_bf, c1, a, r2, b, h, final_w, *, tq, te, final_norm):
    t, d = hn.shape
    n_experts = u_bf.shape[0]
    nh, n_keys, _ = r2.shape
    rows = te // n_keys
    return pl.pallas_call(
        functools.partial(_expert_kernel, final_norm=final_norm),
        grid=(t // tq, n_experts // te),
        in_specs=[pl.BlockSpec((tq, d), lambda i, j: (i, 0)),
                  pl.BlockSpec((te, d), lambda i, j: (j, 0)),
                  pl.BlockSpec((d, te), lambda i, j: (0, j)),
                  pl.BlockSpec((nh, rows, tq), lambda i, j: (0, j, i)),
                  pl.BlockSpec((nh, rows, tq), lambda i, j: (0, j, i)),
                  pl.BlockSpec((nh, n_keys, tq), lambda i, j: (0, 0, i)),
                  pl.BlockSpec((nh, n_keys, tq), lambda i, j: (0, 0, i)),
                  pl.BlockSpec((tq, d), lambda i, j: (i, 0)),
                  _const_spec(final_w.shape)],
        out_specs=pl.BlockSpec((tq, d), lambda i, j: (i, 0)),
        out_shape=jax.ShapeDtypeStruct((t, d), F32),
        scratch_shapes=[pltpu.VMEM((d, tq), F32), pltpu.VMEM((te, tq), BF16)],
        compiler_params=pltpu.CompilerParams(dimension_semantics=("arbitrary", "arbitrary"),
                                             vmem_limit_bytes=VMEM_LIMIT_BYTES),
        name="peer_experts",
    )(hn, u_bf, vt_bf, c1, a, r2, b, h, final_w)


def _rope_tables(seq, dim):
    inv_freq = 1.0 / (ROPE_THETA ** (jnp.arange(0, dim, 2, dtype=F32) / dim))
    ang = jnp.arange(seq, dtype=F32)[:, None] * inv_freq[None, :]
    ang = jnp.concatenate([ang, ang], axis=-1)
    return jnp.cos(ang), jnp.sin(ang)


def _rotate_half_columns(w, dim):
    d_in, width = w.shape
    w4 = w.reshape(d_in, width // dim, 2, dim // 2)
    return jnp.concatenate([-w4[:, :, 1:2], w4[:, :, 0:1]], axis=2).reshape(d_in, width)


def _tile(n, want):
    want = min(n, want)
    assert n % want == 0, (n, want)
    return want


def kernel(x, attn_norm_w, w_in, lambda_q1, lambda_k1, lambda_q2, lambda_k2, subln_w, conv_w,
           w_proj_attn, w_proj_conv, w_out, ffn_norm_w, w_query, sub_keys, expert_u, expert_v,
           final_norm_w):
    batch, seq, d = x.shape
    depth = w_in.shape[0]
    t = batch * seq
    head_dim = lambda_q1.shape[-1]
    ev = subln_w.shape[-1]
    attn_w = w_proj_attn.shape[1]
    conv_wd = conv_w.shape[-1]
    n_heads = attn_w // ev
    qk_w = n_heads * 2 * head_dim
    assert 2 * head_dim == LANES and ev == LANES
    assert w_in.shape[-1] == 2 * qk_w + attn_w + 3 * conv_wd + 2 * d
    peer_heads, _, n_keys, half_dim = sub_keys.shape[1:]
    assert n_keys == LANES and half_dim == LANES
    n_experts = expert_u.shape[1]

    cos, sin = _rope_tables(seq, head_dim)
    cos = jnp.concatenate([cos, cos], axis=-1)
    sin = jnp.concatenate([sin, sin], axis=-1)

    tm = _tile(seq, 256)
    tq_attn = _tile(seq, 512)
    tk_attn = _tile(seq, 512)
    tl = _tile(t, LANES)
    tq_peer = _tile(t, 512)
    te = _tile(n_experts, 1024)

    h = x.reshape(t, d)
    for l in range(depth):
        lambda_init = 0.8 - 0.6 * math.exp(-0.3 * l)
        offs = [0]
        for sz in (qk_w, qk_w, attn_w, conv_wd, conv_wd, conv_wd, d, d):
            offs.append(offs[-1] + sz)
        wl = w_in[l]
        wq, wk, wv, wb, wc, wx, wga, wgb = [wl[:, offs[n]:offs[n + 1]] for n in range(8)]
        ws = (wq, _rotate_half_columns(wq, head_dim), wk, _rotate_half_columns(wk, head_dim),
              wv, wb, wc, wx, wga, wgb)
        ws = tuple(w.astype(BF16) for w in ws)

        q, k, v, bg, u, sga, sgb = _inproj(
            h, attn_norm_w[l][None], cos, sin, ws, seq=seq, tm=tm,
            q_scale=1.0 / math.sqrt(head_dim))
        attn = _attention(
            q, k, v, lambda_q1[l][None], lambda_k1[l][None], lambda_q2[l][None],
            lambda_k2[l][None], subln_w[l][None], batch=batch, seq=seq, n_heads=n_heads,
            tq=tq_attn, tk=tk_attn, lambda_init=lambda_init)
        sk = sub_keys[l].reshape(peer_heads * 2, n_keys, half_dim).astype(BF16)
        h, hn, sc = _merge(
            attn, bg, u, sga, sgb, h, conv_w[l], w_proj_attn[l].astype(BF16),
            w_proj_conv[l].astype(BF16), w_out[l].astype(BF16), ffn_norm_w[l][None],
            w_query[l].astype(BF16), sk, seq=seq, tm=tm)
        c1, r2, a, b = _route(sc, tl=tl)
        h = _experts(
            hn, expert_u[l].astype(BF16), expert_v[l].astype(BF16).T, c1, a, r2, b, h,
            final_norm_w[None], tq=tq_peer, te=te, final_norm=(l == depth - 1))
    return h.reshape(batch, seq, d)
```

```python
import functools
import math

import jax
import jax.numpy as jnp
from jax import lax
from jax.experimental import pallas as pl
from jax.experimental.pallas import tpu as pltpu

NORM_EPS = 1e-6
ROPE_THETA = 10000.0
PEER_TOPK = 16
LANES = 128
SUBLANES = 8
VMEM_LIMIT_BYTES = 56 * 1024 * 1024

F32 = jnp.float32
BF16 = jnp.bfloat16
NEG_INF = float("-inf")


def _nt_dot(a, b):
    return lax.dot_general(a, b, (((1,), (1,)), ((), ())), preferred_element_type=F32)


def _dot(a, b):
    return jnp.dot(a, b, preferred_element_type=F32)


def _const_spec(shape):
    nd = len(shape)
    return pl.BlockSpec(shape, lambda *_: (0,) * nd)


def _inproj_kernel(x_ref, nw_ref, cos_ref, sin_ref, wq_ref, wqr_ref, wk_ref, wkr_ref, wv_ref,
                   wb_ref, wc_ref, wx_ref, wga_ref, wgb_ref,
                   q_ref, k_ref, v_ref, bg_ref, u_ref, sga_ref, sgb_ref, *, q_scale):
    x = x_ref[...]
    ms = jnp.mean(x * x, axis=-1, keepdims=True)
    xn = (x * lax.rsqrt(ms + NORM_EPS) * nw_ref[...]).astype(BF16)
    cos = cos_ref[...]
    sin = sin_ref[...]

    def rope(w_ref, wr_ref, o_ref, scale):
        t = _dot(xn, w_ref[...])
        tr = _dot(xn, wr_ref[...])
        for blk in range(t.shape[1] // LANES):
            sl = slice(blk * LANES, (blk + 1) * LANES)
            r = t[:, sl] * cos + tr[:, sl] * sin
            if scale != 1.0:
                r = r * scale
            o_ref[:, sl] = r.astype(o_ref.dtype)

    rope(wq_ref, wqr_ref, q_ref, q_scale)
    rope(wk_ref, wkr_ref, k_ref, 1.0)
    v_ref[...] = _dot(xn, wv_ref[...]).astype(v_ref.dtype)
    bg_ref[...] = _dot(xn, wb_ref[...])
    u_ref[...] = _dot(xn, wc_ref[...]) * _dot(xn, wx_ref[...])
    sga_ref[...] = jax.nn.sigmoid(_dot(xn, wga_ref[...]))
    sgb_ref[...] = jax.nn.sigmoid(_dot(xn, wgb_ref[...]))


def _inproj(xf, norm_w, cos, sin, ws, *, seq, tm, q_scale):
    t, d = xf.shape
    wq, wqr, wk, wkr, wv, wb, wc, wx, wga, wgb = ws
    n_seq_tiles = seq // tm
    row = lambda c: pl.BlockSpec((tm, c), lambda i: (i, 0))
    out_shapes = (
        jax.ShapeDtypeStruct((t, wq.shape[1]), BF16),
        jax.ShapeDtypeStruct((t, wk.shape[1]), BF16),
        jax.ShapeDtypeStruct((t, wv.shape[1]), BF16),
        jax.ShapeDtypeStruct((t, wb.shape[1]), F32),
        jax.ShapeDtypeStruct((t, wc.shape[1]), F32),
        jax.ShapeDtypeStruct((t, wga.shape[1]), F32),
        jax.ShapeDtypeStruct((t, wgb.shape[1]), F32),
    )
    return pl.pallas_call(
        functools.partial(_inproj_kernel, q_scale=q_scale),
        grid=(t // tm,),
        in_specs=[row(d), _const_spec((1, d)),
                  pl.BlockSpec((tm, LANES), lambda i: (i % n_seq_tiles, 0)),
                  pl.BlockSpec((tm, LANES), lambda i: (i % n_seq_tiles, 0))]
                 + [_const_spec(w.shape) for w in ws],
        out_specs=[row(s.shape[1]) for s in out_shapes],
        out_shape=out_shapes,
        compiler_params=pltpu.CompilerParams(dimension_semantics=("arbitrary",),
                                             vmem_limit_bytes=VMEM_LIMIT_BYTES),
        name="inproj",
    )(xf, norm_w, cos, sin, *ws)


def _attn_kernel(lq1_ref, lk1_ref, lq2_ref, lk2_ref, sw_ref, q_ref, k_ref, v_ref, o_ref,
                 *, tk, head_dim, lambda_init):
    tq = q_ref.shape[0]
    seq = k_ref.shape[0]
    q = q_ref[...]
    lane = lax.broadcasted_iota(jnp.int32, q.shape, 1)
    zero = jnp.zeros_like(q)
    q1 = jnp.where(lane < head_dim, q, zero)
    q2 = jnp.where(lane >= head_dim, q, zero)
    lam = (jnp.exp(jnp.sum(lq1_ref[...] * lk1_ref[...], axis=-1, keepdims=True))
           - jnp.exp(jnp.sum(lq2_ref[...] * lk2_ref[...], axis=-1, keepdims=True)) + lambda_init)

    def update(qm, kc, vc, m, l, a):
        s = _nt_dot(qm, kc)
        mn = jnp.maximum(m, jnp.max(s, axis=-1, keepdims=True))
        p = jnp.exp(s - mn)
        al = jnp.exp(m - mn)
        l = al * l + jnp.sum(p, axis=-1, keepdims=True)
        a = al * a + _dot(p.astype(BF16), vc)
        return mn, l, a

    def body(j, carry):
        m1, l1, a1, m2, l2, a2 = carry
        start = pl.multiple_of(j * tk, tk)
        kc = k_ref[pl.ds(start, tk), :]
        vc = v_ref[pl.ds(start, tk), :]
        m1, l1, a1 = update(q1, kc, vc, m1, l1, a1)
        m2, l2, a2 = update(q2, kc, vc, m2, l2, a2)
        return m1, l1, a1, m2, l2, a2

    ev = v_ref.shape[1]
    init = (jnp.full((tq, 1), NEG_INF, F32), jnp.zeros((tq, 1), F32), jnp.zeros((tq, ev), F32),
            jnp.full((tq, 1), NEG_INF, F32), jnp.zeros((tq, 1), F32), jnp.zeros((tq, ev), F32))
    m1, l1, a1, m2, l2, a2 = lax.fori_loop(0, seq // tk, body, init)
    o = a1 / l1 - lam * (a2 / l2)
    ms = jnp.mean(o * o, axis=-1, keepdims=True)
    o = o * lax.rsqrt(ms + NORM_EPS) * sw_ref[...]
    o_ref[...] = (o * (1.0 - lambda_init)).astype(o_ref.dtype)


def _attention(q, k, v, lq1, lk1, lq2, lk2, subln_w, *, batch, seq, n_heads, tq, tk, lambda_init):
    t = q.shape[0]
    ev = v.shape[1] // n_heads
    head_dim = lq1.shape[-1]
    nq = seq // tq
    small = lambda a: _const_spec(a.shape)
    return pl.pallas_call(
        functools.partial(_attn_kernel, tk=tk, head_dim=head_dim, lambda_init=lambda_init),
        grid=(batch, n_heads, nq),
        in_specs=[small(lq1), small(lk1), small(lq2), small(lk2), small(subln_w),
                  pl.BlockSpec((tq, 2 * head_dim), lambda b, h, i: (b * nq + i, h)),
                  pl.BlockSpec((seq, 2 * head_dim), lambda b, h, i: (b, h)),
                  pl.BlockSpec((seq, ev), lambda b, h, i: (b, h))],
        out_specs=pl.BlockSpec((tq, ev), lambda b, h, i: (b * nq + i, h)),
        out_shape=jax.ShapeDtypeStruct((t, v.shape[1]), BF16),
        compiler_params=pltpu.CompilerParams(
            dimension_semantics=("arbitrary", "arbitrary", "arbitrary"),
            vmem_limit_bytes=VMEM_LIMIT_BYTES),
        name="diff_attn",
    )(lq1, lk1, lq2, lk2, subln_w, q, k, v)


def _merge_kernel(attn_ref, bg_ref, u_ref, up_ref, un_ref, sga_ref, sgb_ref, x_ref,
                  cw_ref, wpa_ref, wpb_ref, wo_ref, fw_ref, wqy_ref, sk_ref,
                  h_ref, hn_ref, sc_ref, *, n_seq_tiles):
    i = pl.program_id(0)
    tm = u_ref.shape[0]
    u = u_ref[...]
    row = lax.broadcasted_iota(jnp.int32, u.shape, 0)
    first = (i % n_seq_tiles) == 0
    last = (i % n_seq_tiles) == n_seq_tiles - 1
    prev_row = jnp.where(first, 0.0, up_ref[SUBLANES - 1:SUBLANES, :])
    next_row = jnp.where(last, 0.0, un_ref[0:1, :])
    u_prev = jnp.where(row == 0, prev_row, pltpu.roll(u, 1, axis=0))
    u_next = jnp.where(row == tm - 1, next_row, pltpu.roll(u, tm - 1, axis=0))
    cw = cw_ref[...]
    conv = cw[0:1, :] * u_prev + cw[1:2, :] * u + cw[2:3, :] * u_next
    y_conv = _dot((bg_ref[...] * conv).astype(BF16), wpb_ref[...])
    y_attn = _dot(attn_ref[...], wpa_ref[...])
    merged = sga_ref[...] * y_attn + sgb_ref[...] * y_conv
    h = x_ref[...] + _dot(merged.astype(BF16), wo_ref[...])
    h_ref[...] = h
    ms = jnp.mean(h * h, axis=-1, keepdims=True)
    hn = (h * lax.rsqrt(ms + NORM_EPS) * fw_ref[...]).astype(BF16)
    hn_ref[...] = hn
    qp = _dot(hn, wqy_ref[...]).astype(BF16)
    for g in range(sk_ref.shape[0]):
        sc_ref[g] = _nt_dot(sk_ref[g], qp[:, g * LANES:(g + 1) * LANES])


def _merge(attn, bg, u, sga, sgb, xf, conv_w, wpa, wpb, wo, ffn_w, wqy, sk, *, seq, tm):
    t, d = xf.shape
    cw = u.shape[1]
    n_tiles = t // tm
    n_seq_tiles = seq // tm
    rb = tm // SUBLANES
    n_rb = t // SUBLANES
    row = lambda c: pl.BlockSpec((tm, c), lambda i: (i, 0))
    n_groups, n_keys, _ = sk.shape
    out_shapes = (jax.ShapeDtypeStruct((t, d), F32),
                  jax.ShapeDtypeStruct((t, d), BF16),
                  jax.ShapeDtypeStruct((n_groups, n_keys, t), F32))
    return pl.pallas_call(
        functools.partial(_merge_kernel, n_seq_tiles=n_seq_tiles),
        grid=(n_tiles,),
        in_specs=[row(attn.shape[1]), row(cw), row(cw),
                  pl.BlockSpec((SUBLANES, cw), lambda i: (jnp.maximum(i * rb - 1, 0), 0)),
                  pl.BlockSpec((SUBLANES, cw), lambda i: (jnp.minimum((i + 1) * rb, n_rb - 1), 0)),
                  row(d), row(d), row(d),
                  _const_spec(conv_w.shape), _const_spec(wpa.shape), _const_spec(wpb.shape),
                  _const_spec(wo.shape), _const_spec(ffn_w.shape), _const_spec(wqy.shape),
                  _const_spec(sk.shape)],
        out_specs=[row(d), row(d), pl.BlockSpec((n_groups, n_keys, tm), lambda i: (0, 0, i))],
        out_shape=out_shapes,
        compiler_params=pltpu.CompilerParams(dimension_semantics=("arbitrary",),
                                             vmem_limit_bytes=VMEM_LIMIT_BYTES),
        name="merge",
    )(attn, bg, u, u, u, sga, sgb, xf, conv_w, wpa, wpb, wo, ffn_w, wqy, sk)


def _top16(s):
    n = s.shape[0]
    idx = lax.broadcasted_iota(jnp.int32, s.shape, 0)
    rank = jnp.full(s.shape, PEER_TOPK, jnp.int32)
    cur = s
    vals = []
    for r in range(PEER_TOPK):
        mx = jnp.max(cur, axis=0, keepdims=True)
        first = jnp.min(jnp.where(cur == mx, idx, n), axis=0, keepdims=True)
        hit = idx == first
        rank = jnp.where(hit, r, rank)
        cur = jnp.where(hit, NEG_INF, cur)
        vals.append(mx)
    return rank, vals


def _route_kernel(sc_ref, c1_ref, r2_ref, a_ref, b_ref):
    tl = sc_ref.shape[-1]
    k = PEER_TOPK
    i16 = lax.broadcasted_iota(jnp.int32, (k, tl), 0)
    i8 = lax.broadcasted_iota(jnp.int32, (SUBLANES, tl), 0)
    order = [i16] + [r1 * k + i8 for r1 in range(1, 8)] + [(8 + i8) * k]
    order = jnp.concatenate(order, axis=0)
    valid = [i16 >= 0] + [i8 < (k // (r1 + 1)) for r1 in range(1, 8)] + [i8 >= 0]
    valid = jnp.concatenate(valid, axis=0)
    big = 2 * k * k

    def head(h, carry):
        s1 = sc_ref[2 * h]
        s2 = sc_ref[2 * h + 1]
        rank1, v1 = _top16(s1)
        rank2, v2 = _top16(s2)
        v2a = jnp.concatenate(v2, axis=0)
        v1b = jnp.concatenate(v1[8:], axis=0)
        e1 = [jnp.exp(v - v1[0]) for v in v1]
        e2a = jnp.exp(v2a - v2[0])
        e1b = jnp.concatenate(e1[8:], axis=0)
        cand = ([v1[0] + v2a] + [v1[r1] + v2a[:SUBLANES] for r1 in range(1, 8)]
                + [v1b + v2[0]])
        cand = jnp.where(valid, jnp.concatenate(cand, axis=0), NEG_INF)
        prod = ([e1[0] * e2a] + [e1[r1] * e2a[:SUBLANES] for r1 in range(1, 8)]
                + [e1b * e2a[0:1]])
        prod = jnp.concatenate(prod, axis=0)
        taken = jnp.zeros(cand.shape, F32)
        cur = cand
        for _ in range(k):
            mx = jnp.max(cur, axis=0, keepdims=True)
            first = jnp.min(jnp.where(cur == mx, order, big), axis=0, keepdims=True)
            hit = order == first
            taken = jnp.where(hit, 1.0, taken)
            cur = jnp.where(hit, NEG_INF, cur)
        z = jnp.sum(taken * prod, axis=0, keepdims=True)
        counts = [jnp.sum(taken[0:k], axis=0, keepdims=True)]
        for r1 in range(1, 8):
            lo = k + (r1 - 1) * SUBLANES
            counts.append(jnp.sum(taken[lo:lo + SUBLANES], axis=0, keepdims=True))
        lo = k + 7 * SUBLANES
        counts += [taken[lo + j:lo + j + 1] for j in range(8)]
        c1 = jnp.zeros(s1.shape, F32)
        for r in range(k):
            c1 = jnp.where(rank1 == r, counts[r], c1)
        c1_ref[h] = c1
        r2_ref[h] = rank2.astype(F32).astype(BF16)
        a_ref[h] = jnp.exp(s1 - v1[0]) / z
        b_ref[h] = jnp.exp(s2 - v2[0]).astype(BF16)
        return carry

    lax.fori_loop(0, sc_ref.shape[0] // 2, head, 0)


def _route(sc, *, tl):
    n_groups, n_keys, t = sc.shape
    nh = n_groups // 2
    words = jax.ShapeDtypeStruct((nh, n_keys, t), F32)
    halfs = jax.ShapeDtypeStruct((nh, n_keys, t), BF16)
    spec = pl.BlockSpec((nh, n_keys, tl), lambda i: (0, 0, i))
    return pl.pallas_call(
        _route_kernel,
        grid=(t // tl,),
        in_specs=[pl.BlockSpec((n_groups, n_keys, tl), lambda i: (0, 0, i))],
        out_specs=[spec] * 4,
        out_shape=(words, halfs, words, halfs),
        compiler_params=pltpu.CompilerParams(dimension_semantics=("arbitrary",),
                                             vmem_limit_bytes=VMEM_LIMIT_BYTES),
        name="peer_route",
    )(sc)


def _bf16_rows(row, n_rows):
    return jnp.broadcast_to(row.astype(BF16), (n_rows, row.shape[1]))


def _expert_kernel(hn_ref, u_ref, vt_ref, c1_ref, a_ref, r2_ref, b_ref, h_ref, fw_ref,
                   o_ref, acc_ref, wt_ref, ht0_ref, ht1_ref, *, n_j, final_norm, lane_chunk):
    s = pl.program_id(0)
    n_heads, n_keys, tq = r2_ref.shape
    rows = c1_ref.shape[1]
    j_prev = jnp.maximum(s - 1, 0) % n_j

    @pl.when(s == 0)
    def _():
        ht1_ref[...] = jnp.zeros_like(ht1_ref)

    @pl.when(j_prev == 0)
    def _():
        acc_ref[...] = jnp.zeros_like(acc_ref)

    def step(ht_w_ref, ht_r_ref):
        ht_w_ref[...] = _nt_dot(u_ref[...], hn_ref[...])
        for r in range(rows):
            x = ht_r_ref[r * n_keys:(r + 1) * n_keys, :]
            act = (x * (0.5 + 0.5 * lax.erf(x * (1.0 / math.sqrt(2.0))))).astype(BF16)
            for t0 in range(0, tq, lane_chunk):
                tsl = slice(t0, t0 + lane_chunk)
                g = None
                for hd in range(n_heads):
                    sel = r2_ref[hd, :, tsl] < _bf16_rows(c1_ref[hd, r:r + 1, tsl], n_keys)
                    term = (jnp.where(sel, b_ref[hd, :, tsl], jnp.zeros((), BF16))
                            * _bf16_rows(a_ref[hd, r:r + 1, tsl], n_keys))
                    g = term if g is None else g + term
                wt_ref[r * n_keys:(r + 1) * n_keys, tsl] = g * act[:, tsl]
        acc_ref[...] += _dot(vt_ref[...], wt_ref[...])

    @pl.when(s % 2 == 0)
    def _():
        step(ht0_ref, ht1_ref)

    @pl.when(s % 2 == 1)
    def _():
        step(ht1_ref, ht0_ref)

    @pl.when(jnp.logical_and(j_prev == n_j - 1, s > 0))
    def _():
        h = h_ref[...] + acc_ref[...].T
        if final_norm:
            ms = jnp.mean(h * h, axis=-1, keepdims=True)
            h = h * lax.rsqrt(ms + NORM_EPS) * fw_ref[...]
        o_ref[...] = h


def _experts(hn, u_bf, vt_bf, c1, a, r2, b, h, final_w, *, tq, te, lane_chunk, final_norm):
    t, d = hn.shape
    n_experts = u_bf.shape[0]
    nh, n_keys, _ = r2.shape
    rows = te // n_keys
    n_j = n_experts // te
    n_tiles = (t // tq) * n_j
    cur = lambda s: jnp.minimum(s, n_tiles - 1)
    prev = lambda s: jnp.maximum(s - 1, 0)
    return pl.pallas_call(
        functools.partial(_expert_kernel, n_j=n_j, final_norm=final_norm, lane_chunk=lane_chunk),
        grid=(n_tiles + 1,),
        in_specs=[pl.BlockSpec((tq, d), lambda s: (cur(s) // n_j, 0)),
                  pl.BlockSpec((te, d), lambda s: (cur(s) % n_j, 0)),
                  pl.BlockSpec((d, te), lambda s: (0, prev(s) % n_j)),
                  pl.BlockSpec((nh, rows, tq), lambda s: (0, prev(s) % n_j, prev(s) // n_j)),
                  pl.BlockSpec((nh, rows, tq), lambda s: (0, prev(s) % n_j, prev(s) // n_j)),
                  pl.BlockSpec((nh, n_keys, tq), lambda s: (0, 0, prev(s) // n_j)),
                  pl.BlockSpec((nh, n_keys, tq), lambda s: (0, 0, prev(s) // n_j)),
                  pl.BlockSpec((tq, d), lambda s: (prev(s) // n_j, 0)),
                  _const_spec(final_w.shape)],
        out_specs=pl.BlockSpec((tq, d), lambda s: (prev(s) // n_j, 0)),
        out_shape=jax.ShapeDtypeStruct((t, d), F32),
        scratch_shapes=[pltpu.VMEM((d, tq), F32), pltpu.VMEM((te, tq), BF16),
                        pltpu.VMEM((te, tq), F32), pltpu.VMEM((te, tq), F32)],
        compiler_params=pltpu.CompilerParams(dimension_semantics=("arbitrary",),
                                             vmem_limit_bytes=VMEM_LIMIT_BYTES),
        name="peer_experts",
    )(hn, u_bf, vt_bf, c1, a, r2, b, h, final_w)


def _rope_tables(seq, dim):
    inv_freq = 1.0 / (ROPE_THETA ** (jnp.arange(0, dim, 2, dtype=F32) / dim))
    ang = jnp.arange(seq, dtype=F32)[:, None] * inv_freq[None, :]
    ang = jnp.concatenate([ang, ang], axis=-1)
    return jnp.cos(ang), jnp.sin(ang)


def _rotate_half_columns(w, dim):
    d_in, width = w.shape
    w4 = w.reshape(d_in, width // dim, 2, dim // 2)
    return jnp.concatenate([-w4[:, :, 1:2], w4[:, :, 0:1]], axis=2).reshape(d_in, width)


def _tile(n, want):
    want = min(n, want)
    assert n % want == 0, (n, want)
    return want


def kernel(x, attn_norm_w, w_in, lambda_q1, lambda_k1, lambda_q2, lambda_k2, subln_w, conv_w,
           w_proj_attn, w_proj_conv, w_out, ffn_norm_w, w_query, sub_keys, expert_u, expert_v,
           final_norm_w):
    batch, seq, d = x.shape
    depth = w_in.shape[0]
    t = batch * seq
    head_dim = lambda_q1.shape[-1]
    ev = subln_w.shape[-1]
    attn_w = w_proj_attn.shape[1]
    conv_wd = conv_w.shape[-1]
    n_heads = attn_w // ev
    qk_w = n_heads * 2 * head_dim
    assert 2 * head_dim == LANES and ev == LANES
    assert w_in.shape[-1] == 2 * qk_w + attn_w + 3 * conv_wd + 2 * d
    peer_heads, _, n_keys, half_dim = sub_keys.shape[1:]
    assert n_keys == LANES and half_dim == LANES
    n_experts = expert_u.shape[1]

    cos, sin = _rope_tables(seq, head_dim)
    cos = jnp.concatenate([cos, cos], axis=-1)
    sin = jnp.concatenate([sin, sin], axis=-1)

    tm = _tile(seq, 256)
    tq_attn = _tile(seq, 512)
    tk_attn = _tile(seq, 512)
    tl = _tile(t, LANES)
    tq_peer = _tile(t, 512)
    te = _tile(n_experts, 1024)

    h = x.reshape(t, d)
    for l in range(depth):
        lambda_init = 0.8 - 0.6 * math.exp(-0.3 * l)
        offs = [0]
        for sz in (qk_w, qk_w, attn_w, conv_wd, conv_wd, conv_wd, d, d):
            offs.append(offs[-1] + sz)
        wl = w_in[l]
        wq, wk, wv, wb, wc, wx, wga, wgb = [wl[:, offs[n]:offs[n + 1]] for n in range(8)]
        ws = (wq, _rotate_half_columns(wq, head_dim), wk, _rotate_half_columns(wk, head_dim),
              wv, wb, wc, wx, wga, wgb)
        ws = tuple(w.astype(BF16) for w in ws)

        q, k, v, bg, u, sga, sgb = _inproj(
            h, attn_norm_w[l][None], cos, sin, ws, seq=seq, tm=tm,
            q_scale=1.0 / math.sqrt(head_dim))
        attn = _attention(
            q, k, v, lambda_q1[l][None], lambda_k1[l][None], lambda_q2[l][None],
            lambda_k2[l][None], subln_w[l][None], batch=batch, seq=seq, n_heads=n_heads,
            tq=tq_attn, tk=tk_attn, lambda_init=lambda_init)
        sk = sub_keys[l].reshape(peer_heads * 2, n_keys, half_dim).astype(BF16)
        h, hn, sc = _merge(
            attn, bg, u, sga, sgb, h, conv_w[l], w_proj_attn[l].astype(BF16),
            w_proj_conv[l].astype(BF16), w_out[l].astype(BF16), ffn_norm_w[l][None],
            w_query[l].astype(BF16), sk, seq=seq, tm=tm)
        c1, r2, a, b = _route(sc, tl=tl)
        h = _experts(
            hn, expert_u[l].astype(BF16), expert_v[l].astype(BF16).T, c1, a, r2, b, h,
            final_norm_w[None], tq=tq_peer, te=te, lane_chunk=_tile(tq_peer, 256),
            final_norm=(l == depth - 1))
    return h.reshape(batch, seq, d)
```

```python
import functools
import math

import jax
import jax.numpy as jnp
from jax import lax
from jax.experimental import pallas as pl
from jax.experimental.pallas import tpu as pltpu

NORM_EPS = 1e-6
ROPE_THETA = 10000.0
PEER_TOPK = 16
LANES = 128
SUBLANES = 8
VMEM_LIMIT_BYTES = 56 * 1024 * 1024

F32 = jnp.float32
BF16 = jnp.bfloat16
NEG_INF = float("-inf")


def _nt_dot(a, b):
    return lax.dot_general(a, b, (((1,), (1,)), ((), ())), preferred_element_type=F32)


def _dot(a, b):
    return jnp.dot(a, b, preferred_element_type=F32)


def _const_spec(shape):
    nd = len(shape)
    return pl.BlockSpec(shape, lambda *_: (0,) * nd)


def _inproj_kernel(x_ref, nw_ref, cos_ref, sin_ref, wq_ref, wqr_ref, wk_ref, wkr_ref, wv_ref,
                   wb_ref, wc_ref, wx_ref, wga_ref, wgb_ref,
                   q_ref, k_ref, v_ref, bg_ref, u_ref, sga_ref, sgb_ref, *, q_scale):
    x = x_ref[...]
    ms = jnp.mean(x * x, axis=-1, keepdims=True)
    xn = (x * lax.rsqrt(ms + NORM_EPS) * nw_ref[...]).astype(BF16)
    cos = cos_ref[...]
    sin = sin_ref[...]

    def rope(w_ref, wr_ref, o_ref, scale):
        t = _dot(xn, w_ref[...])
        tr = _dot(xn, wr_ref[...])
        for blk in range(t.shape[1] // LANES):
            sl = slice(blk * LANES, (blk + 1) * LANES)
            r = t[:, sl] * cos + tr[:, sl] * sin
            if scale != 1.0:
                r = r * scale
            o_ref[:, sl] = r.astype(o_ref.dtype)

    rope(wq_ref, wqr_ref, q_ref, q_scale)
    rope(wk_ref, wkr_ref, k_ref, 1.0)
    v_ref[...] = _dot(xn, wv_ref[...]).astype(v_ref.dtype)
    bg_ref[...] = _dot(xn, wb_ref[...])
    u_ref[...] = _dot(xn, wc_ref[...]) * _dot(xn, wx_ref[...])
    sga_ref[...] = jax.nn.sigmoid(_dot(xn, wga_ref[...]))
    sgb_ref[...] = jax.nn.sigmoid(_dot(xn, wgb_ref[...]))


def _inproj(xf, norm_w, cos, sin, ws, *, seq, tm, q_scale):
    t, d = xf.shape
    wq, wqr, wk, wkr, wv, wb, wc, wx, wga, wgb = ws
    n_seq_tiles = seq // tm
    row = lambda c: pl.BlockSpec((tm, c), lambda i: (i, 0))
    out_shapes = (
        jax.ShapeDtypeStruct((t, wq.shape[1]), BF16),
        jax.ShapeDtypeStruct((t, wk.shape[1]), BF16),
        jax.ShapeDtypeStruct((t, wv.shape[1]), BF16),
        jax.ShapeDtypeStruct((t, wb.shape[1]), F32),
        jax.ShapeDtypeStruct((t, wc.shape[1]), F32),
        jax.ShapeDtypeStruct((t, wga.shape[1]), F32),
        jax.ShapeDtypeStruct((t, wgb.shape[1]), F32),
    )
    return pl.pallas_call(
        functools.partial(_inproj_kernel, q_scale=q_scale),
        grid=(t // tm,),
        in_specs=[row(d), _const_spec((1, d)),
                  pl.BlockSpec((tm, LANES), lambda i: (i % n_seq_tiles, 0)),
                  pl.BlockSpec((tm, LANES), lambda i: (i % n_seq_tiles, 0))]
                 + [_const_spec(w.shape) for w in ws],
        out_specs=[row(s.shape[1]) for s in out_shapes],
        out_shape=out_shapes,
        compiler_params=pltpu.CompilerParams(dimension_semantics=("arbitrary",),
                                             vmem_limit_bytes=VMEM_LIMIT_BYTES),
        name="inproj",
    )(xf, norm_w, cos, sin, *ws)


def _attn_kernel(lq1_ref, lk1_ref, lq2_ref, lk2_ref, sw_ref, q_ref, k_ref, v_ref, o_ref,
                 *, tk, unroll, head_dim, lambda_init):
    tq = q_ref.shape[0]
    seq = k_ref.shape[0]
    q = q_ref[...]
    lane = lax.broadcasted_iota(jnp.int32, q.shape, 1)
    zero = jnp.zeros_like(q)
    q1 = jnp.where(lane < head_dim, q, zero)
    q2 = jnp.where(lane >= head_dim, q, zero)
    lam = (jnp.exp(jnp.sum(lq1_ref[...] * lk1_ref[...], axis=-1, keepdims=True))
           - jnp.exp(jnp.sum(lq2_ref[...] * lk2_ref[...], axis=-1, keepdims=True)) + lambda_init)

    n_lane_blocks = tk // LANES

    def lane_fold(op, acc, s):
        for c in range(n_lane_blocks):
            acc = op(acc, s[:, c * LANES:(c + 1) * LANES])
        return acc

    def max_body(j, carry):
        mp1, mp2 = carry
        kc = k_ref[pl.ds(pl.multiple_of(j * tk, tk), tk), :]
        mp1 = lane_fold(jnp.maximum, mp1, _nt_dot(q1, kc))
        mp2 = lane_fold(jnp.maximum, mp2, _nt_dot(q2, kc))
        return mp1, mp2

    neg = jnp.full((tq, LANES), NEG_INF, F32)
    mp1, mp2 = lax.fori_loop(0, seq // tk, max_body, (neg, neg), unroll=unroll)
    m1 = jnp.max(mp1, axis=-1, keepdims=True)
    m2 = jnp.max(mp2, axis=-1, keepdims=True)

    def acc_body(j, carry):
        lp1, a1, lp2, a2 = carry
        start = pl.multiple_of(j * tk, tk)
        kc = k_ref[pl.ds(start, tk), :]
        vc = v_ref[pl.ds(start, tk), :]
        p1 = jnp.exp(_nt_dot(q1, kc) - m1)
        p2 = jnp.exp(_nt_dot(q2, kc) - m2)
        return (lane_fold(jnp.add, lp1, p1), a1 + _dot(p1.astype(BF16), vc),
                lane_fold(jnp.add, lp2, p2), a2 + _dot(p2.astype(BF16), vc))

    ev = v_ref.shape[1]
    zl = jnp.zeros((tq, LANES), F32)
    za = jnp.zeros((tq, ev), F32)
    lp1, a1, lp2, a2 = lax.fori_loop(0, seq // tk, acc_body, (zl, za, zl, za), unroll=unroll)
    l1 = jnp.sum(lp1, axis=-1, keepdims=True)
    l2 = jnp.sum(lp2, axis=-1, keepdims=True)
    o = a1 / l1 - lam * (a2 / l2)
    ms = jnp.mean(o * o, axis=-1, keepdims=True)
    o = o * lax.rsqrt(ms + NORM_EPS) * sw_ref[...]
    o_ref[...] = (o * (1.0 - lambda_init)).astype(o_ref.dtype)


def _attention(q, k, v, lq1, lk1, lq2, lk2, subln_w, *, batch, seq, n_heads, tq, tk, lambda_init):
    t = q.shape[0]
    ev = v.shape[1] // n_heads
    head_dim = lq1.shape[-1]
    nq = seq // tq
    small = lambda a: _const_spec(a.shape)
    return pl.pallas_call(
        functools.partial(_attn_kernel, tk=tk, unroll=True, head_dim=head_dim,
                          lambda_init=lambda_init),
        grid=(batch, n_heads, nq),
        in_specs=[small(lq1), small(lk1), small(lq2), small(lk2), small(subln_w),
                  pl.BlockSpec((tq, 2 * head_dim), lambda b, h, i: (b * nq + i, h)),
                  pl.BlockSpec((seq, 2 * head_dim), lambda b, h, i: (b, h)),
                  pl.BlockSpec((seq, ev), lambda b, h, i: (b, h))],
        out_specs=pl.BlockSpec((tq, ev), lambda b, h, i: (b * nq + i, h)),
        out_shape=jax.ShapeDtypeStruct((t, v.shape[1]), BF16),
        compiler_params=pltpu.CompilerParams(
            dimension_semantics=("arbitrary", "arbitrary", "arbitrary"),
            vmem_limit_bytes=VMEM_LIMIT_BYTES),
        name="diff_attn",
    )(lq1, lk1, lq2, lk2, subln_w, q, k, v)


def _merge_kernel(attn_ref, bg_ref, u_ref, up_ref, un_ref, sga_ref, sgb_ref, x_ref,
                  cw_ref, wpa_ref, wpb_ref, wo_ref, fw_ref, wqy_ref, sk_ref,
                  h_ref, hn_ref, sc_ref, *, n_seq_tiles):
    i = pl.program_id(0)
    tm = u_ref.shape[0]
    u = u_ref[...]
    row = lax.broadcasted_iota(jnp.int32, u.shape, 0)
    first = (i % n_seq_tiles) == 0
    last = (i % n_seq_tiles) == n_seq_tiles - 1
    prev_row = jnp.where(first, 0.0, up_ref[SUBLANES - 1:SUBLANES, :])
    next_row = jnp.where(last, 0.0, un_ref[0:1, :])
    u_prev = jnp.where(row == 0, prev_row, pltpu.roll(u, 1, axis=0))
    u_next = jnp.where(row == tm - 1, next_row, pltpu.roll(u, tm - 1, axis=0))
    cw = cw_ref[...]
    conv = cw[0:1, :] * u_prev + cw[1:2, :] * u + cw[2:3, :] * u_next
    y_conv = _dot((bg_ref[...] * conv).astype(BF16), wpb_ref[...])
    y_attn = _dot(attn_ref[...], wpa_ref[...])
    merged = sga_ref[...] * y_attn + sgb_ref[...] * y_conv
    h = x_ref[...] + _dot(merged.astype(BF16), wo_ref[...])
    h_ref[...] = h
    ms = jnp.mean(h * h, axis=-1, keepdims=True)
    hn = (h * lax.rsqrt(ms + NORM_EPS) * fw_ref[...]).astype(BF16)
    hn_ref[...] = hn
    qp = _dot(hn, wqy_ref[...]).astype(BF16)
    for g in range(sk_ref.shape[0]):
        sc_ref[g] = _nt_dot(sk_ref[g], qp[:, g * LANES:(g + 1) * LANES])


def _merge(attn, bg, u, sga, sgb, xf, conv_w, wpa, wpb, wo, ffn_w, wqy, sk, *, seq, tm):
    t, d = xf.shape
    cw = u.shape[1]
    n_tiles = t // tm
    n_seq_tiles = seq // tm
    rb = tm // SUBLANES
    n_rb = t // SUBLANES
    row = lambda c: pl.BlockSpec((tm, c), lambda i: (i, 0))
    n_groups, n_keys, _ = sk.shape
    out_shapes = (jax.ShapeDtypeStruct((t, d), F32),
                  jax.ShapeDtypeStruct((t, d), BF16),
                  jax.ShapeDtypeStruct((n_groups, n_keys, t), F32))
    return pl.pallas_call(
        functools.partial(_merge_kernel, n_seq_tiles=n_seq_tiles),
        grid=(n_tiles,),
        in_specs=[row(attn.shape[1]), row(cw), row(cw),
                  pl.BlockSpec((SUBLANES, cw), lambda i: (jnp.maximum(i * rb - 1, 0), 0)),
                  pl.BlockSpec((SUBLANES, cw), lambda i: (jnp.minimum((i + 1) * rb, n_rb - 1), 0)),
                  row(d), row(d), row(d),
                  _const_spec(conv_w.shape), _const_spec(wpa.shape), _const_spec(wpb.shape),
                  _const_spec(wo.shape), _const_spec(ffn_w.shape), _const_spec(wqy.shape),
                  _const_spec(sk.shape)],
        out_specs=[row(d), row(d), pl.BlockSpec((n_groups, n_keys, tm), lambda i: (0, 0, i))],
        out_shape=out_shapes,
        compiler_params=pltpu.CompilerParams(dimension_semantics=("arbitrary",),
                                             vmem_limit_bytes=VMEM_LIMIT_BYTES),
        name="merge",
    )(attn, bg, u, u, u, sga, sgb, xf, conv_w, wpa, wpb, wo, ffn_w, wqy, sk)


def _top16(s, break_ties):
    n = s.shape[0]
    k = PEER_TOPK
    idx = lax.broadcasted_iota(jnp.int32, s.shape, 0)
    rank = jnp.full(s.shape, k, jnp.int32)
    cur = s
    vals = []
    for r in range(k):
        mx = jnp.max(cur, axis=0, keepdims=True)
        hit = cur == mx
        if break_ties:
            first = jnp.min(jnp.where(hit, idx, n), axis=0, keepdims=True)
            hit = idx == first
        rank = jnp.where(hit, r, rank)
        cur = jnp.where(hit, NEG_INF, cur)
        vals.append(mx)
    one_each = k * n - (k * (k + 1)) // 2
    tied = jnp.sum(rank, axis=0, keepdims=True) != one_each
    return rank, vals, tied


def _route_kernel(sc_ref, c1_ref, r2_ref, a_ref, b_ref):
    tied = _route_pass(sc_ref, c1_ref, r2_ref, a_ref, b_ref, break_ties=False)

    @pl.when(jnp.max(tied.astype(F32)) > 0.0)
    def _():
        _route_pass(sc_ref, c1_ref, r2_ref, a_ref, b_ref, break_ties=True)


def _route_pass(sc_ref, c1_ref, r2_ref, a_ref, b_ref, *, break_ties):
    tl = sc_ref.shape[-1]
    k = PEER_TOPK
    i16 = lax.broadcasted_iota(jnp.int32, (k, tl), 0)
    i8 = lax.broadcasted_iota(jnp.int32, (SUBLANES, tl), 0)
    order = [i16] + [r1 * k + i8 for r1 in range(1, 8)] + [(8 + i8) * k]
    order = jnp.concatenate(order, axis=0)
    valid = [i16 >= 0] + [i8 < (k // (r1 + 1)) for r1 in range(1, 8)] + [i8 >= 0]
    valid = jnp.concatenate(valid, axis=0)
    big = 2 * k * k

    def head(h, tied):
        s1 = sc_ref[2 * h]
        s2 = sc_ref[2 * h + 1]
        rank1, v1, tied1 = _top16(s1, break_ties)
        rank2, v2, tied2 = _top16(s2, break_ties)
        v2a = jnp.concatenate(v2, axis=0)
        v1b = jnp.concatenate(v1[8:], axis=0)
        e1 = [jnp.exp(v - v1[0]) for v in v1]
        e2a = jnp.exp(v2a - v2[0])
        e1b = jnp.concatenate(e1[8:], axis=0)
        cand = ([v1[0] + v2a] + [v1[r1] + v2a[:SUBLANES] for r1 in range(1, 8)]
                + [v1b + v2[0]])
        cand = jnp.where(valid, jnp.concatenate(cand, axis=0), NEG_INF)
        prod = ([e1[0] * e2a] + [e1[r1] * e2a[:SUBLANES] for r1 in range(1, 8)]
                + [e1b * e2a[0:1]])
        prod = jnp.concatenate(prod, axis=0)
        taken = jnp.zeros(cand.shape, F32)
        cur = cand
        for _ in range(k):
            mx = jnp.max(cur, axis=0, keepdims=True)
            hit = cur == mx
            if break_ties:
                first = jnp.min(jnp.where(hit, order, big), axis=0, keepdims=True)
                hit = order == first
            taken = jnp.where(hit, 1.0, taken)
            cur = jnp.where(hit, NEG_INF, cur)
        z = jnp.sum(taken * prod, axis=0, keepdims=True)
        counts = [jnp.sum(taken[0:k], axis=0, keepdims=True)]
        for r1 in range(1, 8):
            lo = k + (r1 - 1) * SUBLANES
            counts.append(jnp.sum(taken[lo:lo + SUBLANES], axis=0, keepdims=True))
        lo = k + 7 * SUBLANES
        counts += [taken[lo + j:lo + j + 1] for j in range(8)]
        c1 = jnp.zeros(s1.shape, F32)
        for r in range(k):
            c1 = jnp.where(rank1 == r, counts[r], c1)
        c1_ref[h] = c1
        r2_ref[h] = rank2.astype(F32).astype(BF16)
        a_ref[h] = jnp.exp(s1 - v1[0]) / z
        b_ref[h] = jnp.exp(s2 - v2[0]).astype(BF16)
        n_taken = counts[0]
        for c in counts[1:]:
            n_taken = n_taken + c
        tied_now = jnp.logical_or(jnp.logical_or(tied1, tied2), n_taken != float(k))
        return jnp.maximum(tied, jnp.where(tied_now, 1.0, 0.0))

    return lax.fori_loop(0, sc_ref.shape[0] // 2, head, jnp.zeros((1, tl), F32))


def _route(sc, *, tl):
    n_groups, n_keys, t = sc.shape
    nh = n_groups // 2
    words = jax.ShapeDtypeStruct((nh, n_keys, t), F32)
    halfs = jax.ShapeDtypeStruct((nh, n_keys, t), BF16)
    spec = pl.BlockSpec((nh, n_keys, tl), lambda i: (0, 0, i))
    return pl.pallas_call(
        _route_kernel,
        grid=(t // tl,),
        in_specs=[pl.BlockSpec((n_groups, n_keys, tl), lambda i: (0, 0, i))],
        out_specs=[spec] * 4,
        out_shape=(words, halfs, words, halfs),
        compiler_params=pltpu.CompilerParams(dimension_semantics=("arbitrary",),
                                             vmem_limit_bytes=VMEM_LIMIT_BYTES),
        name="peer_route",
    )(sc)


def _bf16_rows(row, n_rows):
    return jnp.broadcast_to(row.astype(BF16), (n_rows, row.shape[1]))


def _expert_kernel(hn_ref, u_ref, vt_ref, c1_ref, a_ref, r2_ref, b_ref, h_ref, fw_ref,
                   o_ref, acc_ref, wt_ref, ht0_ref, ht1_ref, *, n_j, final_norm, lane_chunk):
    s = pl.program_id(0)
    n_heads, n_keys, tq = r2_ref.shape
    rows = c1_ref.shape[1]
    j_prev = jnp.maximum(s - 1, 0) % n_j

    @pl.when(s == 0)
    def _():
        ht1_ref[...] = jnp.zeros_like(ht1_ref)

    @pl.when(j_prev == 0)
    def _():
        acc_ref[...] = jnp.zeros_like(acc_ref)

    def step(ht_w_ref, ht_r_ref):
        ht_w_ref[...] = _nt_dot(u_ref[...], hn_ref[...])
        for r in range(rows):
            x = ht_r_ref[r * n_keys:(r + 1) * n_keys, :]
            act = (x * (0.5 + 0.5 * lax.erf(x * (1.0 / math.sqrt(2.0))))).astype(BF16)
            for t0 in range(0, tq, lane_chunk):
                tsl = slice(t0, t0 + lane_chunk)
                g = None
                for hd in range(n_heads):
                    sel = r2_ref[hd, :, tsl] < _bf16_rows(c1_ref[hd, r:r + 1, tsl], n_keys)
                    term = (jnp.where(sel, b_ref[hd, :, tsl], jnp.zeros((), BF16))
                            * _bf16_rows(a_ref[hd, r:r + 1, tsl], n_keys))
                    g = term if g is None else g + term
                wt_ref[r * n_keys:(r + 1) * n_keys, tsl] = g * act[:, tsl]
        acc_ref[...] += _dot(vt_ref[...], wt_ref[...])

    @pl.when(s % 2 == 0)
    def _():
        step(ht0_ref, ht1_ref)

    @pl.when(s % 2 == 1)
    def _():
        step(ht1_ref, ht0_ref)

    @pl.when(jnp.logical_and(j_prev == n_j - 1, s > 0))
    def _():
        h = h_ref[...] + acc_ref[...].T
        if final_norm:
            ms = jnp.mean(h * h, axis=-1, keepdims=True)
            h = h * lax.rsqrt(ms + NORM_EPS) * fw_ref[...]
        o_ref[...] = h


def _experts(hn, u_bf, vt_bf, c1, a, r2, b, h, final_w, *, tq, te, lane_chunk, final_norm):
    t, d = hn.shape
    n_experts = u_bf.shape[0]
    nh, n_keys, _ = r2.shape
    rows = te // n_keys
    n_j = n_experts // te
    n_tiles = (t // tq) * n_j
    cur = lambda s: jnp.minimum(s, n_tiles - 1)
    prev = lambda s: jnp.maximum(s - 1, 0)
    return pl.pallas_call(
        functools.partial(_expert_kernel, n_j=n_j, final_norm=final_norm, lane_chunk=lane_chunk),
        grid=(n_tiles + 1,),
        in_specs=[pl.BlockSpec((tq, d), lambda s: (cur(s) // n_j, 0)),
                  pl.BlockSpec((te, d), lambda s: (cur(s) % n_j, 0)),
                  pl.BlockSpec((d, te), lambda s: (0, prev(s) % n_j)),
                  pl.BlockSpec((nh, rows, tq), lambda s: (0, prev(s) % n_j, prev(s) // n_j)),
                  pl.BlockSpec((nh, rows, tq), lambda s: (0, prev(s) % n_j, prev(s) // n_j)),
                  pl.BlockSpec((nh, n_keys, tq), lambda s: (0, 0, prev(s) // n_j)),
                  pl.BlockSpec((nh, n_keys, tq), lambda s: (0, 0, prev(s) // n_j)),
                  pl.BlockSpec((tq, d), lambda s: (prev(s) // n_j, 0)),
                  _const_spec(final_w.shape)],
        out_specs=pl.BlockSpec((tq, d), lambda s: (prev(s) // n_j, 0)),
        out_shape=jax.ShapeDtypeStruct((t, d), F32),
        scratch_shapes=[pltpu.VMEM((d, tq), F32), pltpu.VMEM((te, tq), BF16),
                        pltpu.VMEM((te, tq), F32), pltpu.VMEM((te, tq), F32)],
        compiler_params=pltpu.CompilerParams(dimension_semantics=("arbitrary",),
                                             vmem_limit_bytes=VMEM_LIMIT_BYTES),
        name="peer_experts",
    )(hn, u_bf, vt_bf, c1, a, r2, b, h, final_w)


def _rope_tables(seq, dim):
    inv_freq = 1.0 / (ROPE_THETA ** (jnp.arange(0, dim, 2, dtype=F32) / dim))
    ang = jnp.arange(seq, dtype=F32)[:, None] * inv_freq[None, :]
    ang = jnp.concatenate([ang, ang], axis=-1)
    return jnp.cos(ang), jnp.sin(ang)


def _rotate_half_columns(w, dim):
    d_in, width = w.shape
    w4 = w.reshape(d_in, width // dim, 2, dim // 2)
    return jnp.concatenate([-w4[:, :, 1:2], w4[:, :, 0:1]], axis=2).reshape(d_in, width)


def _tile(n, want):
    want = min(n, want)
    assert n % want == 0, (n, want)
    return want


def kernel(x, attn_norm_w, w_in, lambda_q1, lambda_k1, lambda_q2, lambda_k2, subln_w, conv_w,
           w_proj_attn, w_proj_conv, w_out, ffn_norm_w, w_query, sub_keys, expert_u, expert_v,
           final_norm_w):
    batch, seq, d = x.shape
    depth = w_in.shape[0]
    t = batch * seq
    head_dim = lambda_q1.shape[-1]
    ev = subln_w.shape[-1]
    attn_w = w_proj_attn.shape[1]
    conv_wd = conv_w.shape[-1]
    n_heads = attn_w // ev
    qk_w = n_heads * 2 * head_dim
    assert 2 * head_dim == LANES and ev == LANES
    assert w_in.shape[-1] == 2 * qk_w + attn_w + 3 * conv_wd + 2 * d
    peer_heads, _, n_keys, half_dim = sub_keys.shape[1:]
    assert n_keys == LANES and half_dim == LANES
    n_experts = expert_u.shape[1]

    cos, sin = _rope_tables(seq, head_dim)
    cos = jnp.concatenate([cos, cos], axis=-1)
    sin = jnp.concatenate([sin, sin], axis=-1)

    tm = _tile(seq, 256)
    tq_attn = _tile(seq, 512)
    tk_attn = _tile(seq, 512)
    tl = _tile(t, LANES)
    tq_peer = _tile(t, 512)
    te = _tile(n_experts, 1024)

    h = x.reshape(t, d)
    for l in range(depth):
        lambda_init = 0.8 - 0.6 * math.exp(-0.3 * l)
        offs = [0]
        for sz in (qk_w, qk_w, attn_w, conv_wd, conv_wd, conv_wd, d, d):
            offs.append(offs[-1] + sz)
        wl = w_in[l]
        wq, wk, wv, wb, wc, wx, wga, wgb = [wl[:, offs[n]:offs[n + 1]] for n in range(8)]
        ws = (wq, _rotate_half_columns(wq, head_dim), wk, _rotate_half_columns(wk, head_dim),
              wv, wb, wc, wx, wga, wgb)
        ws = tuple(w.astype(BF16) for w in ws)

        q, k, v, bg, u, sga, sgb = _inproj(
            h, attn_norm_w[l][None], cos, sin, ws, seq=seq, tm=tm,
            q_scale=1.0 / math.sqrt(head_dim))
        attn = _attention(
            q, k, v, lambda_q1[l][None], lambda_k1[l][None], lambda_q2[l][None],
            lambda_k2[l][None], subln_w[l][None], batch=batch, seq=seq, n_heads=n_heads,
            tq=tq_attn, tk=tk_attn, lambda_init=lambda_init)
        sk = sub_keys[l].reshape(peer_heads * 2, n_keys, half_dim).astype(BF16)
        h, hn, sc = _merge(
            attn, bg, u, sga, sgb, h, conv_w[l], w_proj_attn[l].astype(BF16),
            w_proj_conv[l].astype(BF16), w_out[l].astype(BF16), ffn_norm_w[l][None],
            w_query[l].astype(BF16), sk, seq=seq, tm=tm)
        c1, r2, a, b = _route(sc, tl=tl)
        h = _experts(
            hn, expert_u[l].astype(BF16), expert_v[l].astype(BF16).T, c1, a, r2, b, h,
            final_norm_w[None], tq=tq_peer, te=te, lane_chunk=_tile(tq_peer, 256),
            final_norm=(l == depth - 1))
    return h.reshape(batch, seq, d)
```

```python
import functools
import math

import jax
import jax.numpy as jnp
from jax import lax
from jax.experimental import pallas as pl
from jax.experimental.pallas import tpu as pltpu

NORM_EPS = 1e-6
ROPE_THETA = 10000.0
PEER_TOPK = 16
LANES = 128
SUBLANES = 8
VMEM_LIMIT_BYTES = 56 * 1024 * 1024

F32 = jnp.float32
BF16 = jnp.bfloat16
NEG_INF = float("-inf")


def _nt_dot(a, b):
    return lax.dot_general(a, b, (((1,), (1,)), ((), ())), preferred_element_type=F32)


def _dot(a, b):
    return jnp.dot(a, b, preferred_element_type=F32)


def _const_spec(shape):
    nd = len(shape)
    return pl.BlockSpec(shape, lambda *_: (0,) * nd)


def _inproj_kernel(x_ref, nw_ref, cos_ref, sin_ref, wq_ref, wqr_ref, wk_ref, wkr_ref, wv_ref,
                   wb_ref, wc_ref, wx_ref, wga_ref, wgb_ref,
                   q_ref, k_ref, v_ref, bg_ref, u_ref, sga_ref, sgb_ref, *, q_scale):
    x = x_ref[...]
    ms = jnp.mean(x * x, axis=-1, keepdims=True)
    xn = (x * lax.rsqrt(ms + NORM_EPS) * nw_ref[...]).astype(BF16)
    cos = cos_ref[...]
    sin = sin_ref[...]

    def rope(w_ref, wr_ref, o_ref, scale):
        t = _dot(xn, w_ref[...])
        tr = _dot(xn, wr_ref[...])
        for blk in range(t.shape[1] // LANES):
            sl = slice(blk * LANES, (blk + 1) * LANES)
            r = t[:, sl] * cos + tr[:, sl] * sin
            if scale != 1.0:
                r = r * scale
            o_ref[:, sl] = r.astype(o_ref.dtype)

    rope(wq_ref, wqr_ref, q_ref, q_scale)
    rope(wk_ref, wkr_ref, k_ref, 1.0)
    v_ref[...] = _dot(xn, wv_ref[...]).astype(v_ref.dtype)
    bg_ref[...] = _dot(xn, wb_ref[...])
    u_ref[...] = _dot(xn, wc_ref[...]) * _dot(xn, wx_ref[...])
    sga_ref[...] = jax.nn.sigmoid(_dot(xn, wga_ref[...]))
    sgb_ref[...] = jax.nn.sigmoid(_dot(xn, wgb_ref[...]))


def _inproj(xf, norm_w, cos, sin, ws, *, seq, tm, q_scale):
    t, d = xf.shape
    wq, wqr, wk, wkr, wv, wb, wc, wx, wga, wgb = ws
    n_seq_tiles = seq // tm
    row = lambda c: pl.BlockSpec((tm, c), lambda i: (i, 0))
    out_shapes = (
        jax.ShapeDtypeStruct((t, wq.shape[1]), BF16),
        jax.ShapeDtypeStruct((t, wk.shape[1]), BF16),
        jax.ShapeDtypeStruct((t, wv.shape[1]), BF16),
        jax.ShapeDtypeStruct((t, wb.shape[1]), F32),
        jax.ShapeDtypeStruct((t, wc.shape[1]), F32),
        jax.ShapeDtypeStruct((t, wga.shape[1]), F32),
        jax.ShapeDtypeStruct((t, wgb.shape[1]), F32),
    )
    return pl.pallas_call(
        functools.partial(_inproj_kernel, q_scale=q_scale),
        grid=(t // tm,),
        in_specs=[row(d), _const_spec((1, d)),
                  pl.BlockSpec((tm, LANES), lambda i: (i % n_seq_tiles, 0)),
                  pl.BlockSpec((tm, LANES), lambda i: (i % n_seq_tiles, 0))]
                 + [_const_spec(w.shape) for w in ws],
        out_specs=[row(s.shape[1]) for s in out_shapes],
        out_shape=out_shapes,
        compiler_params=pltpu.CompilerParams(dimension_semantics=("arbitrary",),
                                             vmem_limit_bytes=VMEM_LIMIT_BYTES),
        name="inproj",
    )(xf, norm_w, cos, sin, *ws)


def _attn_kernel(lq1_ref, lk1_ref, lq2_ref, lk2_ref, sw_ref, q_ref, k_ref, v_ref, o_ref,
                 *, tk, unroll, head_dim, lambda_init):
    tq = q_ref.shape[0]
    seq = k_ref.shape[0]
    q = q_ref[...]
    lane = lax.broadcasted_iota(jnp.int32, q.shape, 1)
    zero = jnp.zeros_like(q)
    q1 = jnp.where(lane < head_dim, q, zero)
    q2 = jnp.where(lane >= head_dim, q, zero)
    lam = (jnp.exp(jnp.sum(lq1_ref[...] * lk1_ref[...], axis=-1, keepdims=True))
           - jnp.exp(jnp.sum(lq2_ref[...] * lk2_ref[...], axis=-1, keepdims=True)) + lambda_init)

    n_lane_blocks = tk // LANES

    def lane_fold(op, acc, s):
        for c in range(n_lane_blocks):
            acc = op(acc, s[:, c * LANES:(c + 1) * LANES])
        return acc

    def max_body(j, carry):
        mp1, mp2 = carry
        kc = k_ref[pl.ds(pl.multiple_of(j * tk, tk), tk), :]
        mp1 = lane_fold(jnp.maximum, mp1, _nt_dot(q1, kc))
        mp2 = lane_fold(jnp.maximum, mp2, _nt_dot(q2, kc))
        return mp1, mp2

    neg = jnp.full((tq, LANES), NEG_INF, F32)
    mp1, mp2 = lax.fori_loop(0, seq // tk, max_body, (neg, neg), unroll=unroll)
    m1 = jnp.max(mp1, axis=-1, keepdims=True)
    m2 = jnp.max(mp2, axis=-1, keepdims=True)

    def acc_body(j, carry):
        lp1, a1, lp2, a2 = carry
        start = pl.multiple_of(j * tk, tk)
        kc = k_ref[pl.ds(start, tk), :]
        vc = v_ref[pl.ds(start, tk), :]
        p1 = jnp.exp(_nt_dot(q1, kc) - m1)
        p2 = jnp.exp(_nt_dot(q2, kc) - m2)
        return (lane_fold(jnp.add, lp1, p1), a1 + _dot(p1.astype(BF16), vc),
                lane_fold(jnp.add, lp2, p2), a2 + _dot(p2.astype(BF16), vc))

    ev = v_ref.shape[1]
    zl = jnp.zeros((tq, LANES), F32)
    za = jnp.zeros((tq, ev), F32)
    lp1, a1, lp2, a2 = lax.fori_loop(0, seq // tk, acc_body, (zl, za, zl, za), unroll=unroll)
    l1 = jnp.sum(lp1, axis=-1, keepdims=True)
    l2 = jnp.sum(lp2, axis=-1, keepdims=True)
    o = a1 / l1 - lam * (a2 / l2)
    ms = jnp.mean(o * o, axis=-1, keepdims=True)
    o = o * lax.rsqrt(ms + NORM_EPS) * sw_ref[...]
    o_ref[...] = (o * (1.0 - lambda_init)).astype(o_ref.dtype)


def _attention(q, k, v, lq1, lk1, lq2, lk2, subln_w, *, batch, seq, n_heads, tq, tk, lambda_init):
    t = q.shape[0]
    ev = v.shape[1] // n_heads
    head_dim = lq1.shape[-1]
    nq = seq // tq
    small = lambda a: _const_spec(a.shape)
    return pl.pallas_call(
        functools.partial(_attn_kernel, tk=tk, unroll=True, head_dim=head_dim,
                          lambda_init=lambda_init),
        grid=(batch, n_heads, nq),
        in_specs=[small(lq1), small(lk1), small(lq2), small(lk2), small(subln_w),
                  pl.BlockSpec((tq, 2 * head_dim), lambda b, h, i: (b * nq + i, h)),
                  pl.BlockSpec((seq, 2 * head_dim), lambda b, h, i: (b, h)),
                  pl.BlockSpec((seq, ev), lambda b, h, i: (b, h))],
        out_specs=pl.BlockSpec((tq, ev), lambda b, h, i: (b * nq + i, h)),
        out_shape=jax.ShapeDtypeStruct((t, v.shape[1]), BF16),
        compiler_params=pltpu.CompilerParams(
            dimension_semantics=("arbitrary", "arbitrary", "arbitrary"),
            vmem_limit_bytes=VMEM_LIMIT_BYTES),
        name="diff_attn",
    )(lq1, lk1, lq2, lk2, subln_w, q, k, v)


def _merge_kernel(attn_ref, bg_ref, u_ref, up_ref, un_ref, sga_ref, sgb_ref, x_ref,
                  cw_ref, wpa_ref, wpb_ref, wo_ref, fw_ref, wqy_ref, sk_ref,
                  h_ref, hn_ref, sc_ref, *, n_seq_tiles):
    i = pl.program_id(0)
    tm = u_ref.shape[0]
    u = u_ref[...]
    row = lax.broadcasted_iota(jnp.int32, u.shape, 0)
    first = (i % n_seq_tiles) == 0
    last = (i % n_seq_tiles) == n_seq_tiles - 1
    prev_row = jnp.where(first, 0.0, up_ref[SUBLANES - 1:SUBLANES, :])
    next_row = jnp.where(last, 0.0, un_ref[0:1, :])
    u_prev = jnp.where(row == 0, prev_row, pltpu.roll(u, 1, axis=0))
    u_next = jnp.where(row == tm - 1, next_row, pltpu.roll(u, tm - 1, axis=0))
    cw = cw_ref[...]
    conv = cw[0:1, :] * u_prev + cw[1:2, :] * u + cw[2:3, :] * u_next
    y_conv = _dot((bg_ref[...] * conv).astype(BF16), wpb_ref[...])
    y_attn = _dot(attn_ref[...], wpa_ref[...])
    merged = sga_ref[...] * y_attn + sgb_ref[...] * y_conv
    h = x_ref[...] + _dot(merged.astype(BF16), wo_ref[...])
    h_ref[...] = h
    ms = jnp.mean(h * h, axis=-1, keepdims=True)
    hn_f32 = h * lax.rsqrt(ms + NORM_EPS) * fw_ref[...]
    hn = hn_f32.astype(BF16)
    hn_ref[...] = hn_f32.T.astype(BF16)
    qp = _dot(hn, wqy_ref[...]).astype(BF16)
    for g in range(sk_ref.shape[0]):
        sc_ref[g] = _nt_dot(sk_ref[g], qp[:, g * LANES:(g + 1) * LANES])


def _merge(attn, bg, u, sga, sgb, xf, conv_w, wpa, wpb, wo, ffn_w, wqy, sk, *, seq, tm):
    t, d = xf.shape
    cw = u.shape[1]
    n_tiles = t // tm
    n_seq_tiles = seq // tm
    rb = tm // SUBLANES
    n_rb = t // SUBLANES
    row = lambda c: pl.BlockSpec((tm, c), lambda i: (i, 0))
    n_groups, n_keys, _ = sk.shape
    out_shapes = (jax.ShapeDtypeStruct((t, d), F32),
                  jax.ShapeDtypeStruct((d, t), BF16),
                  jax.ShapeDtypeStruct((n_groups, n_keys, t), F32))
    return pl.pallas_call(
        functools.partial(_merge_kernel, n_seq_tiles=n_seq_tiles),
        grid=(n_tiles,),
        in_specs=[row(attn.shape[1]), row(cw), row(cw),
                  pl.BlockSpec((SUBLANES, cw), lambda i: (jnp.maximum(i * rb - 1, 0), 0)),
                  pl.BlockSpec((SUBLANES, cw), lambda i: (jnp.minimum((i + 1) * rb, n_rb - 1), 0)),
                  row(d), row(d), row(d),
                  _const_spec(conv_w.shape), _const_spec(wpa.shape), _const_spec(wpb.shape),
                  _const_spec(wo.shape), _const_spec(ffn_w.shape), _const_spec(wqy.shape),
                  _const_spec(sk.shape)],
        out_specs=[row(d), pl.BlockSpec((d, tm), lambda i: (0, i)),
                   pl.BlockSpec((n_groups, n_keys, tm), lambda i: (0, 0, i))],
        out_shape=out_shapes,
        compiler_params=pltpu.CompilerParams(dimension_semantics=("arbitrary",),
                                             vmem_limit_bytes=VMEM_LIMIT_BYTES),
        name="merge",
    )(attn, bg, u, u, u, sga, sgb, xf, conv_w, wpa, wpb, wo, ffn_w, wqy, sk)


def _top16(s, break_ties, want_rank=True):
    n = s.shape[0]
    k = PEER_TOPK
    idx = lax.broadcasted_iota(jnp.int32, s.shape, 0)
    rank = jnp.full(s.shape, k, jnp.int32) if want_rank else None
    cur = s
    vals = []
    for r in range(k):
        mx = jnp.max(cur, axis=0, keepdims=True)
        hit = cur == mx
        if break_ties:
            first = jnp.min(jnp.where(hit, idx, n), axis=0, keepdims=True)
            hit = idx == first
        if want_rank:
            rank = jnp.where(hit, r, rank)
        cur = jnp.where(hit, NEG_INF, cur)
        vals.append(mx)
    n_taken = jnp.sum(jnp.where(cur != s, 1.0, 0.0), axis=0, keepdims=True)
    return rank, vals, n_taken != float(k)


def _route_kernel(sc_ref, c1_ref, r2_ref, a_ref, b_ref):
    tied = _route_pass(sc_ref, c1_ref, r2_ref, a_ref, b_ref, break_ties=False)

    @pl.when(jnp.max(tied.astype(F32)) > 0.0)
    def _():
        _route_pass(sc_ref, c1_ref, r2_ref, a_ref, b_ref, break_ties=True)


def _route_pass(sc_ref, c1_ref, r2_ref, a_ref, b_ref, *, break_ties):
    tl = sc_ref.shape[-1]
    k = PEER_TOPK
    i16 = lax.broadcasted_iota(jnp.int32, (k, tl), 0)
    i8 = lax.broadcasted_iota(jnp.int32, (SUBLANES, tl), 0)
    order = [i16] + [r1 * k + i8 for r1 in range(1, 8)] + [(8 + i8) * k]
    order = jnp.concatenate(order, axis=0)
    valid = [i16 >= 0] + [i8 < (k // (r1 + 1)) for r1 in range(1, 8)] + [i8 >= 0]
    valid = jnp.concatenate(valid, axis=0)
    big = 2 * k * k

    def head(h, tied):
        s1 = sc_ref[2 * h]
        s2 = sc_ref[2 * h + 1]
        rank1, v1, tied1 = _top16(s1, break_ties, want_rank=break_ties)
        rank2, v2, tied2 = _top16(s2, break_ties)
        v2a = jnp.concatenate(v2, axis=0)
        v1b = jnp.concatenate(v1[8:], axis=0)
        e1 = [jnp.exp(v - v1[0]) for v in v1]
        e2a = jnp.exp(v2a - v2[0])
        e1b = jnp.concatenate(e1[8:], axis=0)
        cand = ([v1[0] + v2a] + [v1[r1] + v2a[:SUBLANES] for r1 in range(1, 8)]
                + [v1b + v2[0]])
        cand = jnp.where(valid, jnp.concatenate(cand, axis=0), NEG_INF)
        prod = ([e1[0] * e2a] + [e1[r1] * e2a[:SUBLANES] for r1 in range(1, 8)]
                + [e1b * e2a[0:1]])
        prod = jnp.concatenate(prod, axis=0)
        cur = cand
        for _ in range(k):
            mx = jnp.max(cur, axis=0, keepdims=True)
            hit = cur == mx
            if break_ties:
                first = jnp.min(jnp.where(hit, order, big), axis=0, keepdims=True)
                hit = order == first
            cur = jnp.where(hit, NEG_INF, cur)
        taken = jnp.where(cur != cand, 1.0, 0.0)
        z = jnp.sum(taken * prod, axis=0, keepdims=True)
        counts = [jnp.sum(taken[0:k], axis=0, keepdims=True)]
        for r1 in range(1, 8):
            lo = k + (r1 - 1) * SUBLANES
            counts.append(jnp.sum(taken[lo:lo + SUBLANES], axis=0, keepdims=True))
        lo = k + 7 * SUBLANES
        counts += [taken[lo + j:lo + j + 1] for j in range(8)]
        c1 = jnp.zeros(s1.shape, F32)
        for r in range(k):
            at_rank = (rank1 == r) if break_ties else (s1 == v1[r])
            c1 = jnp.where(at_rank, counts[r], c1)
        c1_ref[h] = c1
        r2_ref[h] = rank2.astype(F32).astype(BF16)
        a_ref[h] = jnp.exp(s1 - v1[0]) / z
        b_ref[h] = jnp.exp(s2 - v2[0]).astype(BF16)
        n_taken = counts[0]
        for c in counts[1:]:
            n_taken = n_taken + c
        tied_now = jnp.logical_or(jnp.logical_or(tied1, tied2), n_taken != float(k))
        return jnp.maximum(tied, jnp.where(tied_now, 1.0, 0.0))

    return lax.fori_loop(0, sc_ref.shape[0] // 2, head, jnp.zeros((1, tl), F32),
                         unroll=1 if break_ties else 2)


def _route(sc, *, tl):
    n_groups, n_keys, t = sc.shape
    nh = n_groups // 2
    words = jax.ShapeDtypeStruct((nh, n_keys, t), F32)
    halfs = jax.ShapeDtypeStruct((nh, n_keys, t), BF16)
    spec = pl.BlockSpec((nh, n_keys, tl), lambda i: (0, 0, i))
    return pl.pallas_call(
        _route_kernel,
        grid=(t // tl,),
        in_specs=[pl.BlockSpec((n_groups, n_keys, tl), lambda i: (0, 0, i))],
        out_specs=[spec] * 4,
        out_shape=(words, halfs, words, halfs),
        compiler_params=pltpu.CompilerParams(dimension_semantics=("arbitrary",),
                                             vmem_limit_bytes=VMEM_LIMIT_BYTES),
        name="peer_route",
    )(sc)


def _bf16_rows(row, n_rows):
    packed_rows = 2 * SUBLANES
    tile = jnp.broadcast_to(row, (packed_rows, row.shape[1])).astype(BF16)
    return jnp.concatenate([tile] * (n_rows // packed_rows), axis=0)


def _expert_kernel(hn_ref, u_ref, vt_ref, c1_ref, a_ref, r2_ref, b_ref, h_ref, fw_ref,
                   o_ref, acc_ref, wt_ref, ht0_ref, ht1_ref, *, n_j, final_norm, lane_chunk):
    s = pl.program_id(0)
    n_heads, n_keys, tq = r2_ref.shape
    rows = c1_ref.shape[1]
    j_prev = jnp.maximum(s - 1, 0) % n_j

    @pl.when(s == 0)
    def _():
        ht1_ref[...] = jnp.zeros_like(ht1_ref)

    @pl.when(j_prev == 0)
    def _():
        acc_ref[...] = jnp.zeros_like(acc_ref)

    def step(ht_w_ref, ht_r_ref):
        for t0 in range(0, tq, lane_chunk):
            tsl = slice(t0, t0 + lane_chunk)
            ht_w_ref[:, tsl] = _dot(u_ref[...], hn_ref[:, tsl])
            for r in range(rows):
                x = ht_r_ref[r * n_keys:(r + 1) * n_keys, tsl]
                act = (x * (0.5 + 0.5 * lax.erf(x * (1.0 / math.sqrt(2.0))))).astype(BF16)
                g = None
                for hd in range(n_heads):
                    sel = r2_ref[hd, :, tsl] < _bf16_rows(c1_ref[hd, r:r + 1, tsl], n_keys)
                    term = (jnp.where(sel, b_ref[hd, :, tsl], jnp.zeros((), BF16))
                            * _bf16_rows(a_ref[hd, r:r + 1, tsl], n_keys))
                    g = term if g is None else g + term
                wt_ref[r * n_keys:(r + 1) * n_keys, tsl] = g * act
            acc_ref[:, tsl] += _dot(vt_ref[...], wt_ref[:, tsl])

    @pl.when(s % 2 == 0)
    def _():
        step(ht0_ref, ht1_ref)

    @pl.when(s % 2 == 1)
    def _():
        step(ht1_ref, ht0_ref)

    @pl.when(jnp.logical_and(j_prev == n_j - 1, s > 0))
    def _():
        h = h_ref[...] + acc_ref[...].T
        if final_norm:
            ms = jnp.mean(h * h, axis=-1, keepdims=True)
            h = h * lax.rsqrt(ms + NORM_EPS) * fw_ref[...]
        o_ref[...] = h


def _experts(hn_t, u_bf, vt_bf, c1, a, r2, b, h, final_w, *, tq, te, lane_chunk, final_norm):
    d, t = hn_t.shape
    n_experts = u_bf.shape[0]
    nh, n_keys, _ = r2.shape
    rows = te // n_keys
    n_j = n_experts // te
    n_tiles = (t // tq) * n_j
    cur = lambda s: jnp.minimum(s, n_tiles - 1)
    prev = lambda s: jnp.maximum(s - 1, 0)
    return pl.pallas_call(
        functools.partial(_expert_kernel, n_j=n_j, final_norm=final_norm, lane_chunk=lane_chunk),
        grid=(n_tiles + 1,),
        in_specs=[pl.BlockSpec((d, tq), lambda s: (0, cur(s) // n_j)),
                  pl.BlockSpec((te, d), lambda s: (cur(s) % n_j, 0)),
                  pl.BlockSpec((None, d, te), lambda s: (prev(s) % n_j, 0, 0)),
                  pl.BlockSpec((nh, rows, tq), lambda s: (0, prev(s) % n_j, prev(s) // n_j)),
                  pl.BlockSpec((nh, rows, tq), lambda s: (0, prev(s) % n_j, prev(s) // n_j)),
                  pl.BlockSpec((nh, n_keys, tq), lambda s: (0, 0, prev(s) // n_j)),
                  pl.BlockSpec((nh, n_keys, tq), lambda s: (0, 0, prev(s) // n_j)),
                  pl.BlockSpec((tq, d), lambda s: (prev(s) // n_j, 0)),
                  _const_spec(final_w.shape)],
        out_specs=pl.BlockSpec((tq, d), lambda s: (prev(s) // n_j, 0)),
        out_shape=jax.ShapeDtypeStruct((t, d), F32),
        scratch_shapes=[pltpu.VMEM((d, tq), F32), pltpu.VMEM((te, tq), BF16),
                        pltpu.VMEM((te, tq), F32), pltpu.VMEM((te, tq), F32)],
        compiler_params=pltpu.CompilerParams(dimension_semantics=("arbitrary",),
                                             vmem_limit_bytes=VMEM_LIMIT_BYTES),
        name="peer_experts",
    )(hn_t, u_bf, vt_bf, c1, a, r2, b, h, final_w)


def _rope_tables(seq, dim):
    inv_freq = 1.0 / (ROPE_THETA ** (jnp.arange(0, dim, 2, dtype=F32) / dim))
    ang = jnp.arange(seq, dtype=F32)[:, None] * inv_freq[None, :]
    ang = jnp.concatenate([ang, ang], axis=-1)
    return jnp.cos(ang), jnp.sin(ang)


def _rotate_half_columns(w, dim):
    d_in, width = w.shape
    w4 = w.reshape(d_in, width // dim, 2, dim // 2)
    return jnp.concatenate([-w4[:, :, 1:2], w4[:, :, 0:1]], axis=2).reshape(d_in, width)


def _tile(n, want):
    want = min(n, want)
    assert n % want == 0, (n, want)
    return want


def kernel(x, attn_norm_w, w_in, lambda_q1, lambda_k1, lambda_q2, lambda_k2, subln_w, conv_w,
           w_proj_attn, w_proj_conv, w_out, ffn_norm_w, w_query, sub_keys, expert_u, expert_v,
           final_norm_w):
    batch, seq, d = x.shape
    depth = w_in.shape[0]
    t = batch * seq
    head_dim = lambda_q1.shape[-1]
    ev = subln_w.shape[-1]
    attn_w = w_proj_attn.shape[1]
    conv_wd = conv_w.shape[-1]
    n_heads = attn_w // ev
    qk_w = n_heads * 2 * head_dim
    assert 2 * head_dim == LANES and ev == LANES
    assert w_in.shape[-1] == 2 * qk_w + attn_w + 3 * conv_wd + 2 * d
    peer_heads, _, n_keys, half_dim = sub_keys.shape[1:]
    assert n_keys == LANES and half_dim == LANES
    n_experts = expert_u.shape[1]

    cos, sin = _rope_tables(seq, head_dim)
    cos = jnp.concatenate([cos, cos], axis=-1)
    sin = jnp.concatenate([sin, sin], axis=-1)

    tm = _tile(seq, 256)
    tq_attn = _tile(seq, 512)
    tk_attn = _tile(seq, 512)
    tl = _tile(t, LANES)
    tq_peer = _tile(t, 512)
    te = _tile(n_experts, 1024)

    h = x.reshape(t, d)
    for l in range(depth):
        lambda_init = 0.8 - 0.6 * math.exp(-0.3 * l)
        offs = [0]
        for sz in (qk_w, qk_w, attn_w, conv_wd, conv_wd, conv_wd, d, d):
            offs.append(offs[-1] + sz)
        wl = w_in[l]
        wq, wk, wv, wb, wc, wx, wga, wgb = [wl[:, offs[n]:offs[n + 1]] for n in range(8)]
        ws = (wq, _rotate_half_columns(wq, head_dim), wk, _rotate_half_columns(wk, head_dim),
              wv, wb, wc, wx, wga, wgb)
        ws = tuple(w.astype(BF16) for w in ws)

        q, k, v, bg, u, sga, sgb = _inproj(
            h, attn_norm_w[l][None], cos, sin, ws, seq=seq, tm=tm,
            q_scale=1.0 / math.sqrt(head_dim))
        attn = _attention(
            q, k, v, lambda_q1[l][None], lambda_k1[l][None], lambda_q2[l][None],
            lambda_k2[l][None], subln_w[l][None], batch=batch, seq=seq, n_heads=n_heads,
            tq=tq_attn, tk=tk_attn, lambda_init=lambda_init)
        sk = sub_keys[l].reshape(peer_heads * 2, n_keys, half_dim).astype(BF16)
        h, hn, sc = _merge(
            attn, bg, u, sga, sgb, h, conv_w[l], w_proj_attn[l].astype(BF16),
            w_proj_conv[l].astype(BF16), w_out[l].astype(BF16), ffn_norm_w[l][None],
            w_query[l].astype(BF16), sk, seq=seq, tm=tm)
        c1, r2, a, b = _route(sc, tl=tl)
        vt_tiles = expert_v[l].astype(BF16).reshape(n_experts // te, te, d).transpose(0, 2, 1)
        h = _experts(
            hn, expert_u[l].astype(BF16), vt_tiles, c1, a, r2, b, h,
            final_norm_w[None], tq=tq_peer, te=te, lane_chunk=_tile(tq_peer, 256),
            final_norm=(l == depth - 1))
    return h.reshape(batch, seq, d)
```

```python
import functools
import math

import jax
import jax.numpy as jnp
from jax import lax
from jax.experimental import pallas as pl
from jax.experimental.pallas import tpu as pltpu

NORM_EPS = 1e-6
ROPE_THETA = 10000.0
PEER_TOPK = 16
LANES = 128
SUBLANES = 8
VMEM_LIMIT_BYTES = 56 * 1024 * 1024

F32 = jnp.float32
BF16 = jnp.bfloat16
NEG_INF = float("-inf")


def _nt_dot(a, b):
    return lax.dot_general(a, b, (((1,), (1,)), ((), ())), preferred_element_type=F32)


def _dot(a, b):
    return jnp.dot(a, b, preferred_element_type=F32)


def _const_spec(shape):
    nd = len(shape)
    return pl.BlockSpec(shape, lambda *_: (0,) * nd)


def _inproj_kernel(x_ref, nw_ref, cos_ref, sin_ref, wq_ref, wqr_ref, wk_ref, wkr_ref, wv_ref,
                   wb_ref, wc_ref, wx_ref, wga_ref, wgb_ref,
                   q_ref, k_ref, v_ref, bg_ref, u_ref, sga_ref, sgb_ref, *, q_scale):
    x = x_ref[...]
    ms = jnp.mean(x * x, axis=-1, keepdims=True)
    xn = (x * lax.rsqrt(ms + NORM_EPS) * nw_ref[...]).astype(BF16)
    cos = cos_ref[...]
    sin = sin_ref[...]

    def rope(w_ref, wr_ref, o_ref, scale):
        t = _dot(xn, w_ref[...])
        tr = _dot(xn, wr_ref[...])
        for blk in range(t.shape[1] // LANES):
            sl = slice(blk * LANES, (blk + 1) * LANES)
            r = t[:, sl] * cos + tr[:, sl] * sin
            if scale != 1.0:
                r = r * scale
            o_ref[:, sl] = r.astype(o_ref.dtype)

    rope(wq_ref, wqr_ref, q_ref, q_scale)
    rope(wk_ref, wkr_ref, k_ref, 1.0)
    v_ref[...] = _dot(xn, wv_ref[...]).astype(v_ref.dtype)
    bg_ref[...] = _dot(xn, wb_ref[...])
    u_ref[...] = _dot(xn, wc_ref[...]) * _dot(xn, wx_ref[...])
    sga_ref[...] = jax.nn.sigmoid(_dot(xn, wga_ref[...]))
    sgb_ref[...] = jax.nn.sigmoid(_dot(xn, wgb_ref[...]))


def _inproj(xf, norm_w, cos, sin, ws, *, seq, tm, q_scale):
    t, d = xf.shape
    wq, wqr, wk, wkr, wv, wb, wc, wx, wga, wgb = ws
    n_seq_tiles = seq // tm
    row = lambda c: pl.BlockSpec((tm, c), lambda i: (i, 0))
    out_shapes = (
        jax.ShapeDtypeStruct((t, wq.shape[1]), BF16),
        jax.ShapeDtypeStruct((t, wk.shape[1]), BF16),
        jax.ShapeDtypeStruct((t, wv.shape[1]), BF16),
        jax.ShapeDtypeStruct((t, wb.shape[1]), F32),
        jax.ShapeDtypeStruct((t, wc.shape[1]), F32),
        jax.ShapeDtypeStruct((t, wga.shape[1]), F32),
        jax.ShapeDtypeStruct((t, wgb.shape[1]), F32),
    )
    return pl.pallas_call(
        functools.partial(_inproj_kernel, q_scale=q_scale),
        grid=(t // tm,),
        in_specs=[row(d), _const_spec((1, d)),
                  pl.BlockSpec((tm, LANES), lambda i: (i % n_seq_tiles, 0)),
                  pl.BlockSpec((tm, LANES), lambda i: (i % n_seq_tiles, 0))]
                 + [_const_spec(w.shape) for w in ws],
        out_specs=[row(s.shape[1]) for s in out_shapes],
        out_shape=out_shapes,
        compiler_params=pltpu.CompilerParams(dimension_semantics=("arbitrary",),
                                             vmem_limit_bytes=VMEM_LIMIT_BYTES),
        name="inproj",
    )(xf, norm_w, cos, sin, *ws)


def _attn_kernel(lq1_ref, lk1_ref, lq2_ref, lk2_ref, sw_ref, q_ref, k_ref, v_ref, o_ref,
                 *, tk, unroll, head_dim, lambda_init):
    tq = q_ref.shape[0]
    seq = k_ref.shape[0]
    q = q_ref[...]
    lane = lax.broadcasted_iota(jnp.int32, q.shape, 1)
    zero = jnp.zeros_like(q)
    q1 = jnp.where(lane < head_dim, q, zero)
    q2 = jnp.where(lane >= head_dim, q, zero)
    lam = (jnp.exp(jnp.sum(lq1_ref[...] * lk1_ref[...], axis=-1, keepdims=True))
           - jnp.exp(jnp.sum(lq2_ref[...] * lk2_ref[...], axis=-1, keepdims=True)) + lambda_init)

    n_lane_blocks = tk // LANES

    def lane_fold(op, acc, s):
        for c in range(n_lane_blocks):
            acc = op(acc, s[:, c * LANES:(c + 1) * LANES])
        return acc

    def max_body(j, carry):
        mp1, mp2 = carry
        kc = k_ref[pl.ds(pl.multiple_of(j * tk, tk), tk), :]
        mp1 = lane_fold(jnp.maximum, mp1, _nt_dot(q1, kc))
        mp2 = lane_fold(jnp.maximum, mp2, _nt_dot(q2, kc))
        return mp1, mp2

    neg = jnp.full((tq, LANES), NEG_INF, F32)
    mp1, mp2 = lax.fori_loop(0, seq // tk, max_body, (neg, neg), unroll=unroll)
    m1 = jnp.max(mp1, axis=-1, keepdims=True)
    m2 = jnp.max(mp2, axis=-1, keepdims=True)

    def acc_body(j, carry):
        lp1, a1, lp2, a2 = carry
        start = pl.multiple_of(j * tk, tk)
        kc = k_ref[pl.ds(start, tk), :]
        vc = v_ref[pl.ds(start, tk), :]
        p1 = jnp.exp(_nt_dot(q1, kc) - m1)
        p2 = jnp.exp(_nt_dot(q2, kc) - m2)
        return (lane_fold(jnp.add, lp1, p1), a1 + _dot(p1.astype(BF16), vc),
                lane_fold(jnp.add, lp2, p2), a2 + _dot(p2.astype(BF16), vc))

    ev = v_ref.shape[1]
    zl = jnp.zeros((tq, LANES), F32)
    za = jnp.zeros((tq, ev), F32)
    lp1, a1, lp2, a2 = lax.fori_loop(0, seq // tk, acc_body, (zl, za, zl, za), unroll=unroll)
    l1 = jnp.sum(lp1, axis=-1, keepdims=True)
    l2 = jnp.sum(lp2, axis=-1, keepdims=True)
    o = a1 / l1 - lam * (a2 / l2)
    ms = jnp.mean(o * o, axis=-1, keepdims=True)
    o = o * lax.rsqrt(ms + NORM_EPS) * sw_ref[...]
    o_ref[...] = (o * (1.0 - lambda_init)).astype(o_ref.dtype)


def _attention(q, k, v, lq1, lk1, lq2, lk2, subln_w, *, batch, seq, n_heads, tq, tk, lambda_init):
    t = q.shape[0]
    ev = v.shape[1] // n_heads
    head_dim = lq1.shape[-1]
    nq = seq // tq
    small = lambda a: _const_spec(a.shape)
    return pl.pallas_call(
        functools.partial(_attn_kernel, tk=tk, unroll=True, head_dim=head_dim,
                          lambda_init=lambda_init),
        grid=(batch, n_heads, nq),
        in_specs=[small(lq1), small(lk1), small(lq2), small(lk2), small(subln_w),
                  pl.BlockSpec((tq, 2 * head_dim), lambda b, h, i: (b * nq + i, h)),
                  pl.BlockSpec((seq, 2 * head_dim), lambda b, h, i: (b, h)),
                  pl.BlockSpec((seq, ev), lambda b, h, i: (b, h))],
        out_specs=pl.BlockSpec((tq, ev), lambda b, h, i: (b * nq + i, h)),
        out_shape=jax.ShapeDtypeStruct((t, v.shape[1]), BF16),
        compiler_params=pltpu.CompilerParams(
            dimension_semantics=("arbitrary", "arbitrary", "arbitrary"),
            vmem_limit_bytes=VMEM_LIMIT_BYTES),
        name="diff_attn",
    )(lq1, lk1, lq2, lk2, subln_w, q, k, v)


def _merge_kernel(attn_ref, bg_ref, u_ref, up_ref, un_ref, sga_ref, sgb_ref, x_ref,
                  cw_ref, wpa_ref, wpb_ref, wo_ref, fw_ref, wqy_ref, sk_ref,
                  h_ref, hn_ref, sc_ref, *, n_seq_tiles):
    i = pl.program_id(0)
    tm = u_ref.shape[0]
    u = u_ref[...]
    row = lax.broadcasted_iota(jnp.int32, u.shape, 0)
    first = (i % n_seq_tiles) == 0
    last = (i % n_seq_tiles) == n_seq_tiles - 1
    prev_row = jnp.where(first, 0.0, up_ref[SUBLANES - 1:SUBLANES, :])
    next_row = jnp.where(last, 0.0, un_ref[0:1, :])
    u_prev = jnp.where(row == 0, prev_row, pltpu.roll(u, 1, axis=0))
    u_next = jnp.where(row == tm - 1, next_row, pltpu.roll(u, tm - 1, axis=0))
    cw = cw_ref[...]
    conv = cw[0:1, :] * u_prev + cw[1:2, :] * u + cw[2:3, :] * u_next
    y_conv = _dot((bg_ref[...] * conv).astype(BF16), wpb_ref[...])
    y_attn = _dot(attn_ref[...], wpa_ref[...])
    merged = sga_ref[...] * y_attn + sgb_ref[...] * y_conv
    h = x_ref[...] + _dot(merged.astype(BF16), wo_ref[...])
    h_ref[...] = h
    ms = jnp.mean(h * h, axis=-1, keepdims=True)
    hn_f32 = h * lax.rsqrt(ms + NORM_EPS) * fw_ref[...]
    hn = hn_f32.astype(BF16)
    hn_ref[...] = hn_f32.T.astype(BF16)
    qp = _dot(hn, wqy_ref[...]).astype(BF16)
    for g in range(sk_ref.shape[0]):
        sc_ref[g] = _nt_dot(sk_ref[g], qp[:, g * LANES:(g + 1) * LANES])


def _merge(attn, bg, u, sga, sgb, xf, conv_w, wpa, wpb, wo, ffn_w, wqy, sk, *, seq, tm):
    t, d = xf.shape
    cw = u.shape[1]
    n_tiles = t // tm
    n_seq_tiles = seq // tm
    rb = tm // SUBLANES
    n_rb = t // SUBLANES
    row = lambda c: pl.BlockSpec((tm, c), lambda i: (i, 0))
    n_groups, n_keys, _ = sk.shape
    out_shapes = (jax.ShapeDtypeStruct((t, d), F32),
                  jax.ShapeDtypeStruct((d, t), BF16),
                  jax.ShapeDtypeStruct((n_groups, n_keys, t), F32))
    return pl.pallas_call(
        functools.partial(_merge_kernel, n_seq_tiles=n_seq_tiles),
        grid=(n_tiles,),
        in_specs=[row(attn.shape[1]), row(cw), row(cw),
                  pl.BlockSpec((SUBLANES, cw), lambda i: (jnp.maximum(i * rb - 1, 0), 0)),
                  pl.BlockSpec((SUBLANES, cw), lambda i: (jnp.minimum((i + 1) * rb, n_rb - 1), 0)),
                  row(d), row(d), row(d),
                  _const_spec(conv_w.shape), _const_spec(wpa.shape), _const_spec(wpb.shape),
                  _const_spec(wo.shape), _const_spec(ffn_w.shape), _const_spec(wqy.shape),
                  _const_spec(sk.shape)],
        out_specs=[row(d), pl.BlockSpec((d, tm), lambda i: (0, i)),
                   pl.BlockSpec((n_groups, n_keys, tm), lambda i: (0, 0, i))],
        out_shape=out_shapes,
        compiler_params=pltpu.CompilerParams(dimension_semantics=("arbitrary",),
                                             vmem_limit_bytes=VMEM_LIMIT_BYTES),
        name="merge",
    )(attn, bg, u, u, u, sga, sgb, xf, conv_w, wpa, wpb, wo, ffn_w, wqy, sk)


def _top16(s):
    n = s.shape[0]
    k = PEER_TOPK
    idx = lax.broadcasted_iota(jnp.int32, s.shape, 0)
    rank = jnp.full(s.shape, k, jnp.int32)
    cur = s
    vals = []
    for r in range(k):
        mx = jnp.max(cur, axis=0, keepdims=True)
        first = jnp.min(jnp.where(cur == mx, idx, n), axis=0, keepdims=True)
        hit = idx == first
        rank = jnp.where(hit, r, rank)
        cur = jnp.where(hit, NEG_INF, cur)
        vals.append(mx)
    return rank, vals, jnp.zeros((1, s.shape[1]), jnp.bool_)


def _batcher_sort_network(lo, hi):
    def merge(lo, hi, r):
        step = 2 * r
        if step < hi - lo:
            yield from merge(lo, hi, step)
            yield from merge(lo + r, hi, step)
            yield from ((i, i + r) for i in range(lo + r, hi - r, step))
        else:
            yield (lo, lo + r)
    if hi - lo >= 1:
        mid = lo + (hi - lo) // 2
        yield from _batcher_sort_network(lo, mid)
        yield from _batcher_sort_network(mid + 1, hi)
        yield from merge(lo, hi, 1)


def _bitonic_merge_network(n):
    d = n // 2
    while d >= 1:
        yield from ((i, i + d) for i in range(n) if not i & d)
        d //= 2


_SORT16 = tuple(_batcher_sort_network(0, PEER_TOPK - 1))
_MERGE16 = tuple(_bitonic_merge_network(PEER_TOPK))


def _sublane_pieces(s):
    return [s[i * SUBLANES:(i + 1) * SUBLANES] for i in range(s.shape[0] // SUBLANES)]


def _top16_network(pieces):
    k = PEER_TOPK
    assert len(pieces) == k

    def exchange(rows, network):
        for i, j in network:
            rows[i], rows[j] = jnp.maximum(rows[i], rows[j]), jnp.minimum(rows[i], rows[j])

    rows = list(pieces)
    exchange(rows, _SORT16)
    shift = 1
    while shift < SUBLANES:
        other = [pltpu.roll(x, shift, axis=0) for x in rows]
        rows = [jnp.maximum(rows[i], other[k - 1 - i]) for i in range(k)]
        exchange(rows, _MERGE16)
        shift *= 2
    dup = rows[0] == rows[1]
    for r in range(1, k - 1):
        dup = jnp.logical_or(dup, rows[r] == rows[r + 1])
    n_ge = jnp.zeros(pieces[0].shape, F32)
    for p in pieces:
        n_ge = n_ge + jnp.where(p >= rows[k - 1], 1.0, 0.0)
    n_ge = jnp.sum(n_ge, axis=0, keepdims=True)
    return rows, jnp.logical_or(dup[0:1], n_ge != float(k))


def _rank_by_count(pieces, rows):
    out = []
    for p in pieces:
        rank = jnp.zeros(p.shape, F32)
        for r, v in enumerate(rows):
            rank = jnp.where(v > p, float(r + 1), rank)
        out.append(rank)
    return jnp.concatenate(out, axis=0)


def _route_kernel(sc_ref, c1_ref, r2_ref, a_ref, b_ref):
    tied = _route_pass(sc_ref, c1_ref, r2_ref, a_ref, b_ref, break_ties=False)

    @pl.when(jnp.max(tied.astype(F32)) > 0.0)
    def _():
        _route_pass(sc_ref, c1_ref, r2_ref, a_ref, b_ref, break_ties=True)


def _route_pass(sc_ref, c1_ref, r2_ref, a_ref, b_ref, *, break_ties):
    tl = sc_ref.shape[-1]
    k = PEER_TOPK
    i16 = lax.broadcasted_iota(jnp.int32, (k, tl), 0)
    i8 = lax.broadcasted_iota(jnp.int32, (SUBLANES, tl), 0)
    order = [i16] + [r1 * k + i8 for r1 in range(1, 8)] + [(8 + i8) * k]
    order = jnp.concatenate(order, axis=0)
    valid = [i16 >= 0] + [i8 < (k // (r1 + 1)) for r1 in range(1, 8)] + [i8 >= 0]
    valid = jnp.concatenate(valid, axis=0)
    big = 2 * k * k

    def head(h, tied):
        s1 = sc_ref[2 * h]
        s2 = sc_ref[2 * h + 1]
        if break_ties:
            rank1, v1, tied1 = _top16(s1)
            rank2, v2, tied2 = _top16(s2)
            rank2 = rank2.astype(F32)
        else:
            p1, p2 = _sublane_pieces(s1), _sublane_pieces(s2)
            rows1, tied1 = _top16_network(p1)
            rows2, tied2 = _top16_network(p2)
            v1 = [x[0:1] for x in rows1]
            v2 = [x[0:1] for x in rows2]
            rank2 = _rank_by_count(p2, rows2)
        v2a = jnp.concatenate(v2, axis=0)
        v1b = jnp.concatenate(v1[8:], axis=0)
        e1 = [jnp.exp(v - v1[0]) for v in v1]
        e2a = jnp.exp(v2a - v2[0])
        e1b = jnp.concatenate(e1[8:], axis=0)
        cand = ([v1[0] + v2a] + [v1[r1] + v2a[:SUBLANES] for r1 in range(1, 8)]
                + [v1b + v2[0]])
        cand = jnp.where(valid, jnp.concatenate(cand, axis=0), NEG_INF)
        prod = ([e1[0] * e2a] + [e1[r1] * e2a[:SUBLANES] for r1 in range(1, 8)]
                + [e1b * e2a[0:1]])
        prod = jnp.concatenate(prod, axis=0)
        cur = cand
        for _ in range(k):
            mx = jnp.max(cur, axis=0, keepdims=True)
            hit = cur == mx
            if break_ties:
                first = jnp.min(jnp.where(hit, order, big), axis=0, keepdims=True)
                hit = order == first
            cur = jnp.where(hit, NEG_INF, cur)
        taken = jnp.where(cur != cand, 1.0, 0.0)
        z = jnp.sum(taken * prod, axis=0, keepdims=True)
        counts = [jnp.sum(taken[0:k], axis=0, keepdims=True)]
        for r1 in range(1, 8):
            lo = k + (r1 - 1) * SUBLANES
            counts.append(jnp.sum(taken[lo:lo + SUBLANES], axis=0, keepdims=True))
        lo = k + 7 * SUBLANES
        counts += [taken[lo + j:lo + j + 1] for j in range(8)]
        if break_ties:
            c1 = jnp.zeros(s1.shape, F32)
            for r in range(k):
                c1 = jnp.where(rank1 == r, counts[r], c1)
        else:
            wide = [jnp.broadcast_to(c, (SUBLANES, tl)) for c in counts]
            c1 = []
            for p in p1:
                c = jnp.zeros(p.shape, F32)
                for r in range(k):
                    c = jnp.where(p == rows1[r], wide[r], c)
                c1.append(c)
            c1 = jnp.concatenate(c1, axis=0)
        c1_ref[h] = c1
        r2_ref[h] = rank2.astype(BF16)
        a_ref[h] = jnp.exp(s1 - v1[0]) / z
        b_ref[h] = jnp.exp(s2 - v2[0]).astype(BF16)
        n_taken = counts[0]
        for c in counts[1:]:
            n_taken = n_taken + c
        tied_now = jnp.logical_or(jnp.logical_or(tied1, tied2), n_taken != float(k))
        return jnp.maximum(tied, jnp.where(tied_now, 1.0, 0.0))

    return lax.fori_loop(0, sc_ref.shape[0] // 2, head, jnp.zeros((1, tl), F32),
                         unroll=1 if break_ties else 4)


def _route(sc, *, tl):
    n_groups, n_keys, t = sc.shape
    nh = n_groups // 2
    words = jax.ShapeDtypeStruct((nh, n_keys, t), F32)
    halfs = jax.ShapeDtypeStruct((nh, n_keys, t), BF16)
    spec = pl.BlockSpec((nh, n_keys, tl), lambda i: (0, 0, i))
    return pl.pallas_call(
        _route_kernel,
        grid=(t // tl,),
        in_specs=[pl.BlockSpec((n_groups, n_keys, tl), lambda i: (0, 0, i))],
        out_specs=[spec] * 4,
        out_shape=(words, halfs, words, halfs),
        compiler_params=pltpu.CompilerParams(dimension_semantics=("arbitrary",),
                                             vmem_limit_bytes=VMEM_LIMIT_BYTES),
        name="peer_route",
    )(sc)


def _bf16_rows(row, n_rows):
    packed_rows = 2 * SUBLANES
    tile = jnp.broadcast_to(row, (packed_rows, row.shape[1])).astype(BF16)
    return jnp.concatenate([tile] * (n_rows // packed_rows), axis=0)


def _expert_kernel(hn_ref, u_ref, vt_ref, c1_ref, a_ref, r2_ref, b_ref, h_ref, fw_ref,
                   o_ref, acc_ref, wt_ref, ht0_ref, ht1_ref, *, n_j, final_norm, lane_chunk):
    s = pl.program_id(0)
    n_heads, n_keys, tq = r2_ref.shape
    rows = c1_ref.shape[1]
    j_prev = jnp.maximum(s - 1, 0) % n_j

    @pl.when(s == 0)
    def _():
        ht1_ref[...] = jnp.zeros_like(ht1_ref)

    @pl.when(j_prev == 0)
    def _():
        acc_ref[...] = jnp.zeros_like(acc_ref)

    def step(ht_w_ref, ht_r_ref):
        for t0 in range(0, tq, lane_chunk):
            tsl = slice(t0, t0 + lane_chunk)
            ht_w_ref[:, tsl] = _dot(u_ref[...], hn_ref[:, tsl])
            for r in range(rows):
                x = ht_r_ref[r * n_keys:(r + 1) * n_keys, tsl]
                act = (x * (0.5 + 0.5 * lax.erf(x * (1.0 / math.sqrt(2.0))))).astype(BF16)
                g = None
                for hd in range(n_heads):
                    sel = r2_ref[hd, :, tsl] < _bf16_rows(c1_ref[hd, r:r + 1, tsl], n_keys)
                    term = (jnp.where(sel, b_ref[hd, :, tsl], jnp.zeros((), BF16))
                            * _bf16_rows(a_ref[hd, r:r + 1, tsl], n_keys))
                    g = term if g is None else g + term
                wt_ref[r * n_keys:(r + 1) * n_keys, tsl] = g * act
            acc_ref[:, tsl] += _dot(vt_ref[...], wt_ref[:, tsl])

    @pl.when(s % 2 == 0)
    def _():
        step(ht0_ref, ht1_ref)

    @pl.when(s % 2 == 1)
    def _():
        step(ht1_ref, ht0_ref)

    @pl.when(jnp.logical_and(j_prev == n_j - 1, s > 0))
    def _():
        h = h_ref[...] + acc_ref[...].T
        if final_norm:
            ms = jnp.mean(h * h, axis=-1, keepdims=True)
            h = h * lax.rsqrt(ms + NORM_EPS) * fw_ref[...]
        o_ref[...] = h


def _experts(hn_t, u_bf, vt_bf, c1, a, r2, b, h, final_w, *, tq, te, lane_chunk, final_norm):
    d, t = hn_t.shape
    n_experts = u_bf.shape[0]
    nh, n_keys, _ = r2.shape
    rows = te // n_keys
    n_j = n_experts // te
    n_tiles = (t // tq) * n_j
    cur = lambda s: jnp.minimum(s, n_tiles - 1)
    prev = lambda s: jnp.maximum(s - 1, 0)
    return pl.pallas_call(
        functools.partial(_expert_kernel, n_j=n_j, final_norm=final_norm, lane_chunk=lane_chunk),
        grid=(n_tiles + 1,),
        in_specs=[pl.BlockSpec((d, tq), lambda s: (0, cur(s) // n_j)),
                  pl.BlockSpec((te, d), lambda s: (cur(s) % n_j, 0)),
                  pl.BlockSpec((None, d, te), lambda s: (prev(s) % n_j, 0, 0)),
                  pl.BlockSpec((nh, rows, tq), lambda s: (0, prev(s) % n_j, prev(s) // n_j)),
                  pl.BlockSpec((nh, rows, tq), lambda s: (0, prev(s) % n_j, prev(s) // n_j)),
                  pl.BlockSpec((nh, n_keys, tq), lambda s: (0, 0, prev(s) // n_j)),
                  pl.BlockSpec((nh, n_keys, tq), lambda s: (0, 0, prev(s) // n_j)),
                  pl.BlockSpec((tq, d), lambda s: (prev(s) // n_j, 0)),
                  _const_spec(final_w.shape)],
        out_specs=pl.BlockSpec((tq, d), lambda s: (prev(s) // n_j, 0)),
        out_shape=jax.ShapeDtypeStruct((t, d), F32),
        scratch_shapes=[pltpu.VMEM((d, tq), F32), pltpu.VMEM((te, tq), BF16),
                        pltpu.VMEM((te, tq), F32), pltpu.VMEM((te, tq), F32)],
        compiler_params=pltpu.CompilerParams(dimension_semantics=("arbitrary",),
                                             vmem_limit_bytes=VMEM_LIMIT_BYTES),
        name="peer_experts",
    )(hn_t, u_bf, vt_bf, c1, a, r2, b, h, final_w)


def _rope_tables(seq, dim):
    inv_freq = 1.0 / (ROPE_THETA ** (jnp.arange(0, dim, 2, dtype=F32) / dim))
    ang = jnp.arange(seq, dtype=F32)[:, None] * inv_freq[None, :]
    ang = jnp.concatenate([ang, ang], axis=-1)
    return jnp.cos(ang), jnp.sin(ang)


def _rotate_half_columns(w, dim):
    d_in, width = w.shape
    w4 = w.reshape(d_in, width // dim, 2, dim // 2)
    return jnp.concatenate([-w4[:, :, 1:2], w4[:, :, 0:1]], axis=2).reshape(d_in, width)


def _tile(n, want):
    want = min(n, want)
    assert n % want == 0, (n, want)
    return want


def kernel(x, attn_norm_w, w_in, lambda_q1, lambda_k1, lambda_q2, lambda_k2, subln_w, conv_w,
           w_proj_attn, w_proj_conv, w_out, ffn_norm_w, w_query, sub_keys, expert_u, expert_v,
           final_norm_w):
    batch, seq, d = x.shape
    depth = w_in.shape[0]
    t = batch * seq
    head_dim = lambda_q1.shape[-1]
    ev = subln_w.shape[-1]
    attn_w = w_proj_attn.shape[1]
    conv_wd = conv_w.shape[-1]
    n_heads = attn_w // ev
    qk_w = n_heads * 2 * head_dim
    assert 2 * head_dim == LANES and ev == LANES
    assert w_in.shape[-1] == 2 * qk_w + attn_w + 3 * conv_wd + 2 * d
    peer_heads, _, n_keys, half_dim = sub_keys.shape[1:]
    assert n_keys == LANES and half_dim == LANES
    n_experts = expert_u.shape[1]

    cos, sin = _rope_tables(seq, head_dim)
    cos = jnp.concatenate([cos, cos], axis=-1)
    sin = jnp.concatenate([sin, sin], axis=-1)

    tm = _tile(seq, 256)
    tq_attn = _tile(seq, 512)
    tk_attn = _tile(seq, 512)
    tl = _tile(t, LANES)
    tq_peer = _tile(t, 512)
    te = _tile(n_experts, 1024)

    h = x.reshape(t, d)
    for l in range(depth):
        lambda_init = 0.8 - 0.6 * math.exp(-0.3 * l)
        offs = [0]
        for sz in (qk_w, qk_w, attn_w, conv_wd, conv_wd, conv_wd, d, d):
            offs.append(offs[-1] + sz)
        wl = w_in[l]
        wq, wk, wv, wb, wc, wx, wga, wgb = [wl[:, offs[n]:offs[n + 1]] for n in range(8)]
        ws = (wq, _rotate_half_columns(wq, head_dim), wk, _rotate_half_columns(wk, head_dim),
              wv, wb, wc, wx, wga, wgb)
        ws = tuple(w.astype(BF16) for w in ws)

        q, k, v, bg, u, sga, sgb = _inproj(
            h, attn_norm_w[l][None], cos, sin, ws, seq=seq, tm=tm,
            q_scale=1.0 / math.sqrt(head_dim))
        attn = _attention(
            q, k, v, lambda_q1[l][None], lambda_k1[l][None], lambda_q2[l][None],
            lambda_k2[l][None], subln_w[l][None], batch=batch, seq=seq, n_heads=n_heads,
            tq=tq_attn, tk=tk_attn, lambda_init=lambda_init)
        sk = sub_keys[l].reshape(peer_heads * 2, n_keys, half_dim).astype(BF16)
        h, hn, sc = _merge(
            attn, bg, u, sga, sgb, h, conv_w[l], w_proj_attn[l].astype(BF16),
            w_proj_conv[l].astype(BF16), w_out[l].astype(BF16), ffn_norm_w[l][None],
            w_query[l].astype(BF16), sk, seq=seq, tm=tm)
        c1, r2, a, b = _route(sc, tl=tl)
        vt_tiles = expert_v[l].astype(BF16).reshape(n_experts // te, te, d).transpose(0, 2, 1)
        h = _experts(
            hn, expert_u[l].astype(BF16), vt_tiles, c1, a, r2, b, h,
            final_norm_w[None], tq=tq_peer, te=te, lane_chunk=_tile(tq_peer, 256),
            final_norm=(l == depth - 1))
    return h.reshape(batch, seq, d)
```

```python
import functools
import math

import jax
import jax.numpy as jnp
from jax import lax
from jax.experimental import pallas as pl
from jax.experimental.pallas import tpu as pltpu

NORM_EPS = 1e-6
ROPE_THETA = 10000.0
PEER_TOPK = 16
LANES = 128
SUBLANES = 8
VMEM_LIMIT_BYTES = 56 * 1024 * 1024

F32 = jnp.float32
BF16 = jnp.bfloat16
NEG_INF = float("-inf")


def _nt_dot(a, b):
    return lax.dot_general(a, b, (((1,), (1,)), ((), ())), preferred_element_type=F32)


def _dot(a, b):
    return jnp.dot(a, b, preferred_element_type=F32)


def _const_spec(shape):
    nd = len(shape)
    return pl.BlockSpec(shape, lambda *_: (0,) * nd)


def _inproj_kernel(x_ref, nw_ref, cos_ref, sin_ref, wq_ref, wk_ref, wv_ref,
                   wb_ref, wc_ref, wx_ref, wga_ref, wgb_ref,
                   q_ref, k_ref, v_ref, bg_ref, u_ref, sga_ref, sgb_ref, *, q_scale, half):
    x = x_ref[...]
    ms = jnp.mean(x * x, axis=-1, keepdims=True)
    xn = (x * lax.rsqrt(ms + NORM_EPS) * nw_ref[...]).astype(BF16)
    cos = cos_ref[...]
    sin = sin_ref[...]
    lane = lax.broadcasted_iota(jnp.int32, cos.shape, 1)
    low = (lane % (2 * half)) < half

    def rope(w_ref, o_ref, scale):
        t = _dot(xn, w_ref[...])
        for blk in range(t.shape[1] // LANES):
            sl = slice(blk * LANES, (blk + 1) * LANES)
            tb = t[:, sl]
            partner = jnp.where(low, pltpu.roll(tb, LANES - half, axis=1),
                                pltpu.roll(tb, half, axis=1))
            r = tb * cos + partner * sin
            if scale != 1.0:
                r = r * scale
            o_ref[:, sl] = r.astype(o_ref.dtype)

    rope(wq_ref, q_ref, q_scale)
    rope(wk_ref, k_ref, 1.0)
    v_ref[...] = _dot(xn, wv_ref[...]).astype(v_ref.dtype)
    bg_ref[...] = _dot(xn, wb_ref[...])
    u_ref[...] = _dot(xn, wc_ref[...]) * _dot(xn, wx_ref[...])
    sga_ref[...] = jax.nn.sigmoid(_dot(xn, wga_ref[...]))
    sgb_ref[...] = jax.nn.sigmoid(_dot(xn, wgb_ref[...]))


def _inproj(xf, norm_w, cos, sin, ws, *, seq, tm, q_scale, half):
    t, d = xf.shape
    wq, wk, wv, wb, wc, wx, wga, wgb = ws
    n_seq_tiles = seq // tm
    row = lambda c: pl.BlockSpec((tm, c), lambda i: (i, 0))
    out_shapes = (
        jax.ShapeDtypeStruct((t, wq.shape[1]), BF16),
        jax.ShapeDtypeStruct((t, wk.shape[1]), BF16),
        jax.ShapeDtypeStruct((t, wv.shape[1]), BF16),
        jax.ShapeDtypeStruct((t, wb.shape[1]), F32),
        jax.ShapeDtypeStruct((t, wc.shape[1]), F32),
        jax.ShapeDtypeStruct((t, wga.shape[1]), F32),
        jax.ShapeDtypeStruct((t, wgb.shape[1]), F32),
    )
    return pl.pallas_call(
        functools.partial(_inproj_kernel, q_scale=q_scale, half=half),
        grid=(t // tm,),
        in_specs=[row(d), _const_spec((1, d)),
                  pl.BlockSpec((tm, LANES), lambda i: (i % n_seq_tiles, 0)),
                  pl.BlockSpec((tm, LANES), lambda i: (i % n_seq_tiles, 0))]
                 + [_const_spec(w.shape) for w in ws],
        out_specs=[row(s.shape[1]) for s in out_shapes],
        out_shape=out_shapes,
        compiler_params=pltpu.CompilerParams(dimension_semantics=("arbitrary",),
                                             vmem_limit_bytes=VMEM_LIMIT_BYTES),
        name="inproj",
    )(xf, norm_w, cos, sin, *ws)


def _attn_kernel(lq1_ref, lk1_ref, lq2_ref, lk2_ref, sw_ref, q_ref, k_ref, v_ref, o_ref,
                 *, tk, head_dim, lambda_init):
    tq = q_ref.shape[0]
    seq = k_ref.shape[0]
    q = q_ref[...]
    lane = lax.broadcasted_iota(jnp.int32, q.shape, 1)
    zero = jnp.zeros_like(q)
    q1 = jnp.where(lane < head_dim, q, zero)
    q2 = jnp.where(lane >= head_dim, q, zero)
    lam = (jnp.exp(jnp.sum(lq1_ref[...] * lk1_ref[...], axis=-1, keepdims=True))
           - jnp.exp(jnp.sum(lq2_ref[...] * lk2_ref[...], axis=-1, keepdims=True)) + lambda_init)

    def lane_fold(op, acc, s):
        for c in range(s.shape[1] // LANES):
            acc = op(acc, s[:, c * LANES:(c + 1) * LANES])
        return acc

    def softmax_values(qm):
        parts = []
        for j in range(seq // tk):
            s = _nt_dot(qm, k_ref[j * tk:(j + 1) * tk, :])
            mc = jnp.max(lane_fold(jnp.maximum, s[:, :LANES], s[:, LANES:]), axis=-1, keepdims=True)
            p = jnp.exp(s - mc)
            lsum = lane_fold(jnp.add, p[:, :LANES], p[:, LANES:])
            parts.append((mc, lsum, _dot(p.astype(BF16), v_ref[j * tk:(j + 1) * tk, :])))
        m = parts[0][0]
        for mc, _, _ in parts[1:]:
            m = jnp.maximum(m, mc)
        lsum = acc = None
        for mc, lc, oc in parts:
            w = jnp.exp(mc - m)
            lsum = w * lc if lsum is None else lsum + w * lc
            acc = w * oc if acc is None else acc + w * oc
        return jnp.sum(lsum, axis=-1, keepdims=True), acc

    l1, a1 = softmax_values(q1)
    l2, a2 = softmax_values(q2)
    o = a1 / l1 - lam * (a2 / l2)
    ms = jnp.mean(o * o, axis=-1, keepdims=True)
    o = o * lax.rsqrt(ms + NORM_EPS) * sw_ref[...]
    o_ref[...] = (o * (1.0 - lambda_init)).astype(o_ref.dtype)


def _attention(q, k, v, lq1, lk1, lq2, lk2, subln_w, *, batch, seq, n_heads, tq, tk, lambda_init):
    t = q.shape[0]
    ev = v.shape[1] // n_heads
    head_dim = lq1.shape[-1]
    nq = seq // tq
    small = lambda a: _const_spec(a.shape)
    return pl.pallas_call(
        functools.partial(_attn_kernel, tk=tk, head_dim=head_dim,
                          lambda_init=lambda_init),
        grid=(batch, n_heads, nq),
        in_specs=[small(lq1), small(lk1), small(lq2), small(lk2), small(subln_w),
                  pl.BlockSpec((tq, 2 * head_dim), lambda b, h, i: (b * nq + i, h)),
                  pl.BlockSpec((seq, 2 * head_dim), lambda b, h, i: (b, h)),
                  pl.BlockSpec((seq, ev), lambda b, h, i: (b, h))],
        out_specs=pl.BlockSpec((tq, ev), lambda b, h, i: (b * nq + i, h)),
        out_shape=jax.ShapeDtypeStruct((t, v.shape[1]), BF16),
        compiler_params=pltpu.CompilerParams(
            dimension_semantics=("arbitrary", "arbitrary", "arbitrary"),
            vmem_limit_bytes=VMEM_LIMIT_BYTES),
        name="diff_attn",
    )(lq1, lk1, lq2, lk2, subln_w, q, k, v)


def _merge_kernel(attn_ref, bg_ref, u_ref, up_ref, un_ref, sga_ref, sgb_ref, x_ref,
                  cw_ref, wpa_ref, wpb_ref, wo_ref, fw_ref, wqy_ref, sk_ref,
                  h_ref, hn_ref, sc_ref, *, n_seq_tiles):
    i = pl.program_id(0)
    tm = u_ref.shape[0]
    u = u_ref[...]
    row = lax.broadcasted_iota(jnp.int32, u.shape, 0)
    first = (i % n_seq_tiles) == 0
    last = (i % n_seq_tiles) == n_seq_tiles - 1
    prev_row = jnp.where(first, 0.0, up_ref[SUBLANES - 1:SUBLANES, :])
    next_row = jnp.where(last, 0.0, un_ref[0:1, :])
    u_prev = jnp.where(row == 0, prev_row, pltpu.roll(u, 1, axis=0))
    u_next = jnp.where(row == tm - 1, next_row, pltpu.roll(u, tm - 1, axis=0))
    cw = cw_ref[...]
    conv = cw[0:1, :] * u_prev + cw[1:2, :] * u + cw[2:3, :] * u_next
    y_conv = _dot((bg_ref[...] * conv).astype(BF16), wpb_ref[...])
    y_attn = _dot(attn_ref[...], wpa_ref[...])
    merged = sga_ref[...] * y_attn + sgb_ref[...] * y_conv
    h = x_ref[...] + _dot(merged.astype(BF16), wo_ref[...])
    h_ref[...] = h
    ms = jnp.mean(h * h, axis=-1, keepdims=True)
    hn_f32 = h * lax.rsqrt(ms + NORM_EPS) * fw_ref[...]
    hn = hn_f32.astype(BF16)
    hn_ref[...] = hn_f32.T.astype(BF16)
    qp = _dot(hn, wqy_ref[...]).astype(BF16)
    for g in range(sk_ref.shape[0]):
        sc_ref[g] = _nt_dot(sk_ref[g], qp[:, g * LANES:(g + 1) * LANES])


def _merge(attn, bg, u, sga, sgb, xf, conv_w, wpa, wpb, wo, ffn_w, wqy, sk, *, seq, tm):
    t, d = xf.shape
    cw = u.shape[1]
    n_tiles = t // tm
    n_seq_tiles = seq // tm
    rb = tm // SUBLANES
    n_rb = t // SUBLANES
    row = lambda c: pl.BlockSpec((tm, c), lambda i: (i, 0))
    n_groups, n_keys, _ = sk.shape
    out_shapes = (jax.ShapeDtypeStruct((t, d), F32),
                  jax.ShapeDtypeStruct((d, t), BF16),
                  jax.ShapeDtypeStruct((n_groups, n_keys, t), F32))
    return pl.pallas_call(
        functools.partial(_merge_kernel, n_seq_tiles=n_seq_tiles),
        grid=(n_tiles,),
        in_specs=[row(attn.shape[1]), row(cw), row(cw),
                  pl.BlockSpec((SUBLANES, cw), lambda i: (jnp.maximum(i * rb - 1, 0), 0)),
                  pl.BlockSpec((SUBLANES, cw), lambda i: (jnp.minimum((i + 1) * rb, n_rb - 1), 0)),
                  row(d), row(d), row(d),
                  _const_spec(conv_w.shape), _const_spec(wpa.shape), _const_spec(wpb.shape),
                  _const_spec(wo.shape), _const_spec(ffn_w.shape), _const_spec(wqy.shape),
                  _const_spec(sk.shape)],
        out_specs=[row(d), pl.BlockSpec((d, tm), lambda i: (0, i)),
                   pl.BlockSpec((n_groups, n_keys, tm), lambda i: (0, 0, i))],
        out_shape=out_shapes,
        compiler_params=pltpu.CompilerParams(dimension_semantics=("arbitrary",),
                                             vmem_limit_bytes=VMEM_LIMIT_BYTES),
        name="merge",
    )(attn, bg, u, u, u, sga, sgb, xf, conv_w, wpa, wpb, wo, ffn_w, wqy, sk)


def _top16(s):
    n = s.shape[0]
    k = PEER_TOPK
    idx = lax.broadcasted_iota(jnp.int32, s.shape, 0)
    rank = jnp.full(s.shape, k, jnp.int32)
    cur = s
    vals = []
    for r in range(k):
        mx = jnp.max(cur, axis=0, keepdims=True)
        first = jnp.min(jnp.where(cur == mx, idx, n), axis=0, keepdims=True)
        hit = idx == first
        rank = jnp.where(hit, r, rank)
        cur = jnp.where(hit, NEG_INF, cur)
        vals.append(mx)
    return rank, vals, jnp.zeros((1, s.shape[1]), jnp.bool_)


def _batcher_sort_network(lo, hi):
    def merge(lo, hi, r):
        step = 2 * r
        if step < hi - lo:
            yield from merge(lo, hi, step)
            yield from merge(lo + r, hi, step)
            yield from ((i, i + r) for i in range(lo + r, hi - r, step))
        else:
            yield (lo, lo + r)
    if hi - lo >= 1:
        mid = lo + (hi - lo) // 2
        yield from _batcher_sort_network(lo, mid)
        yield from _batcher_sort_network(mid + 1, hi)
        yield from merge(lo, hi, 1)


def _bitonic_merge_network(n):
    d = n // 2
    while d >= 1:
        yield from ((i, i + d) for i in range(n) if not i & d)
        d //= 2


_SORT16 = tuple(_batcher_sort_network(0, PEER_TOPK - 1))
_MERGE16 = tuple(_bitonic_merge_network(PEER_TOPK))


def _sublane_pieces(s):
    return [s[i * SUBLANES:(i + 1) * SUBLANES] for i in range(s.shape[0] // SUBLANES)]


def _top16_network(pieces):
    k = PEER_TOPK
    assert len(pieces) == k

    def exchange(rows, network):
        for i, j in network:
            rows[i], rows[j] = jnp.maximum(rows[i], rows[j]), jnp.minimum(rows[i], rows[j])

    rows = list(pieces)
    exchange(rows, _SORT16)
    shift = 1
    while shift < SUBLANES:
        other = [pltpu.roll(x, shift, axis=0) for x in rows]
        rows = [jnp.maximum(rows[i], other[k - 1 - i]) for i in range(k)]
        exchange(rows, _MERGE16)
        shift *= 2
    dup = rows[0] == rows[1]
    for r in range(1, k - 1):
        dup = jnp.logical_or(dup, rows[r] == rows[r + 1])
    n_ge = jnp.zeros(pieces[0].shape, F32)
    for p in pieces:
        n_ge = n_ge + jnp.where(p >= rows[k - 1], 1.0, 0.0)
    n_ge = jnp.sum(n_ge, axis=0, keepdims=True)
    return rows, jnp.logical_or(dup[0:1], n_ge != float(k))


def _rank_by_count(pieces, rows):
    out = []
    for p in pieces:
        rank = jnp.zeros(p.shape, F32)
        for r, v in enumerate(rows):
            rank = jnp.where(v > p, float(r + 1), rank)
        out.append(rank)
    return jnp.concatenate(out, axis=0)


def _route_kernel(sc_ref, c1_ref, r2_ref, a_ref, b_ref):
    tied = _route_pass(sc_ref, c1_ref, r2_ref, a_ref, b_ref, break_ties=False)

    @pl.when(jnp.max(tied.astype(F32)) > 0.0)
    def _():
        _route_pass(sc_ref, c1_ref, r2_ref, a_ref, b_ref, break_ties=True)


def _route_pass(sc_ref, c1_ref, r2_ref, a_ref, b_ref, *, break_ties):
    tl = sc_ref.shape[-1]
    k = PEER_TOPK
    i16 = lax.broadcasted_iota(jnp.int32, (k, tl), 0)
    i8 = lax.broadcasted_iota(jnp.int32, (SUBLANES, tl), 0)
    order = [i16] + [r1 * k + i8 for r1 in range(1, 8)] + [(8 + i8) * k]
    order = jnp.concatenate(order, axis=0)
    valid = [i16 >= 0] + [i8 < (k // (r1 + 1)) for r1 in range(1, 8)] + [i8 >= 0]
    valid = jnp.concatenate(valid, axis=0)
    big = 2 * k * k

    def head(h, tied):
        s1 = sc_ref[2 * h]
        s2 = sc_ref[2 * h + 1]
        if break_ties:
            rank1, v1, tied1 = _top16(s1)
            rank2, v2, tied2 = _top16(s2)
            rank2 = rank2.astype(F32)
        else:
            p1, p2 = _sublane_pieces(s1), _sublane_pieces(s2)
            rows1, tied1 = _top16_network(p1)
            rows2, tied2 = _top16_network(p2)
            v1 = [x[0:1] for x in rows1]
            v2 = [x[0:1] for x in rows2]
            rank2 = _rank_by_count(p2, rows2)
        v2a = jnp.concatenate(v2, axis=0)
        v1b = jnp.concatenate(v1[8:], axis=0)
        e1 = [jnp.exp(v - v1[0]) for v in v1]
        e2a = jnp.exp(v2a - v2[0])
        e1b = jnp.concatenate(e1[8:], axis=0)
        cand = ([v1[0] + v2a] + [v1[r1] + v2a[:SUBLANES] for r1 in range(1, 8)]
                + [v1b + v2[0]])
        cand = jnp.where(valid, jnp.concatenate(cand, axis=0), NEG_INF)
        prod = ([e1[0] * e2a] + [e1[r1] * e2a[:SUBLANES] for r1 in range(1, 8)]
                + [e1b * e2a[0:1]])
        prod = jnp.concatenate(prod, axis=0)
        cur = cand
        for _ in range(k):
            mx = jnp.max(cur, axis=0, keepdims=True)
            hit = cur == mx
            if break_ties:
                first = jnp.min(jnp.where(hit, order, big), axis=0, keepdims=True)
                hit = order == first
            cur = jnp.where(hit, NEG_INF, cur)
        taken = jnp.where(cur != cand, 1.0, 0.0)
        z = jnp.sum(taken * prod, axis=0, keepdims=True)
        counts = [jnp.sum(taken[0:k], axis=0, keepdims=True)]
        for r1 in range(1, 8):
            lo = k + (r1 - 1) * SUBLANES
            counts.append(jnp.sum(taken[lo:lo + SUBLANES], axis=0, keepdims=True))
        lo = k + 7 * SUBLANES
        counts += [taken[lo + j:lo + j + 1] for j in range(8)]
        if break_ties:
            c1 = jnp.zeros(s1.shape, F32)
            for r in range(k):
                c1 = jnp.where(rank1 == r, counts[r], c1)
        else:
            wide = [jnp.broadcast_to(c, (SUBLANES, tl)) for c in counts]
            c1 = []
            for p in p1:
                c = jnp.zeros(p.shape, F32)
                for r in range(k):
                    c = jnp.where(p == rows1[r], wide[r], c)
                c1.append(c)
            c1 = jnp.concatenate(c1, axis=0)
        c1_ref[h] = c1
        r2_ref[h] = rank2.astype(BF16)
        a_ref[h] = jnp.exp(s1 - v1[0]) / z
        b_ref[h] = jnp.exp(s2 - v2[0]).astype(BF16)
        n_taken = counts[0]
        for c in counts[1:]:
            n_taken = n_taken + c
        tied_now = jnp.logical_or(jnp.logical_or(tied1, tied2), n_taken != float(k))
        return jnp.maximum(tied, jnp.where(tied_now, 1.0, 0.0))

    return lax.fori_loop(0, sc_ref.shape[0] // 2, head, jnp.zeros((1, tl), F32),
                         unroll=1 if break_ties else 4)


def _route(sc, *, tl):
    n_groups, n_keys, t = sc.shape
    nh = n_groups // 2
    words = jax.ShapeDtypeStruct((nh, n_keys, t), F32)
    halfs = jax.ShapeDtypeStruct((nh, n_keys, t), BF16)
    spec = pl.BlockSpec((nh, n_keys, tl), lambda i: (0, 0, i))
    return pl.pallas_call(
        _route_kernel,
        grid=(t // tl,),
        in_specs=[pl.BlockSpec((n_groups, n_keys, tl), lambda i: (0, 0, i))],
        out_specs=[spec] * 4,
        out_shape=(words, halfs, words, halfs),
        compiler_params=pltpu.CompilerParams(dimension_semantics=("arbitrary",),
                                             vmem_limit_bytes=VMEM_LIMIT_BYTES),
        name="peer_route",
    )(sc)


def _bf16_rows(row, n_rows):
    packed_rows = 2 * SUBLANES
    tile = jnp.broadcast_to(row, (packed_rows, row.shape[1])).astype(BF16)
    return jnp.concatenate([tile] * (n_rows // packed_rows), axis=0)


def _expert_kernel(hn_ref, u_ref, vt_ref, c1_ref, a_ref, r2_ref, b_ref, h_ref, fw_ref,
                   o_ref, acc_ref, wt_ref, ht0_ref, ht1_ref, *, n_j, final_norm, lane_chunk):
    s = pl.program_id(0)
    n_heads, n_keys, tq = r2_ref.shape
    rows = c1_ref.shape[1]
    j_prev = jnp.maximum(s - 1, 0) % n_j

    @pl.when(s == 0)
    def _():
        ht1_ref[...] = jnp.zeros_like(ht1_ref)

    @pl.when(j_prev == 0)
    def _():
        acc_ref[...] = jnp.zeros_like(acc_ref)

    def step(ht_w_ref, ht_r_ref):
        for t0 in range(0, tq, lane_chunk):
            tsl = slice(t0, t0 + lane_chunk)
            ht_w_ref[:, tsl] = _dot(u_ref[...], hn_ref[:, tsl])
            for r in range(rows):
                x = ht_r_ref[r * n_keys:(r + 1) * n_keys, tsl]
                act = (x * (0.5 + 0.5 * lax.erf(x * (1.0 / math.sqrt(2.0))))).astype(BF16)
                g = None
                for hd in range(n_heads):
                    sel = r2_ref[hd, :, tsl] < _bf16_rows(c1_ref[hd, r:r + 1, tsl], n_keys)
                    term = (jnp.where(sel, b_ref[hd, :, tsl], jnp.zeros((), BF16))
                            * _bf16_rows(a_ref[hd, r:r + 1, tsl], n_keys))
                    g = term if g is None else g + term
                wt_ref[r * n_keys:(r + 1) * n_keys, tsl] = g * act
            acc_ref[:, tsl] += _dot(vt_ref[...], wt_ref[:, tsl])

    @pl.when(s % 2 == 0)
    def _():
        step(ht0_ref, ht1_ref)

    @pl.when(s % 2 == 1)
    def _():
        step(ht1_ref, ht0_ref)

    @pl.when(jnp.logical_and(j_prev == n_j - 1, s > 0))
    def _():
        h = h_ref[...] + acc_ref[...].T
        if final_norm:
            ms = jnp.mean(h * h, axis=-1, keepdims=True)
            h = h * lax.rsqrt(ms + NORM_EPS) * fw_ref[...]
        o_ref[...] = h


def _experts(hn_t, u_bf, vt_bf, c1, a, r2, b, h, final_w, *, tq, te, lane_chunk, final_norm):
    d, t = hn_t.shape
    n_experts = u_bf.shape[0]
    nh, n_keys, _ = r2.shape
    rows = te // n_keys
    n_j = n_experts // te
    n_tiles = (t // tq) * n_j
    cur = lambda s: jnp.minimum(s, n_tiles - 1)
    prev = lambda s: jnp.maximum(s - 1, 0)
    return pl.pallas_call(
        functools.partial(_expert_kernel, n_j=n_j, final_norm=final_norm, lane_chunk=lane_chunk),
        grid=(n_tiles + 1,),
        in_specs=[pl.BlockSpec((d, tq), lambda s: (0, cur(s) // n_j)),
                  pl.BlockSpec((te, d), lambda s: (cur(s) % n_j, 0)),
                  pl.BlockSpec((None, d, te), lambda s: (prev(s) % n_j, 0, 0)),
                  pl.BlockSpec((nh, rows, tq), lambda s: (0, prev(s) % n_j, prev(s) // n_j)),
                  pl.BlockSpec((nh, rows, tq), lambda s: (0, prev(s) % n_j, prev(s) // n_j)),
                  pl.BlockSpec((nh, n_keys, tq), lambda s: (0, 0, prev(s) // n_j)),
                  pl.BlockSpec((nh, n_keys, tq), lambda s: (0, 0, prev(s) // n_j)),
                  pl.BlockSpec((tq, d), lambda s: (prev(s) // n_j, 0)),
                  _const_spec(final_w.shape)],
        out_specs=pl.BlockSpec((tq, d), lambda s: (prev(s) // n_j, 0)),
        out_shape=jax.ShapeDtypeStruct((t, d), F32),
        scratch_shapes=[pltpu.VMEM((d, tq), F32), pltpu.VMEM((te, tq), BF16),
                        pltpu.VMEM((te, tq), F32), pltpu.VMEM((te, tq), F32)],
        compiler_params=pltpu.CompilerParams(dimension_semantics=("arbitrary",),
                                             vmem_limit_bytes=VMEM_LIMIT_BYTES),
        name="peer_experts",
    )(hn_t, u_bf, vt_bf, c1, a, r2, b, h, final_w)


def _rope_tables(seq, dim):
    inv_freq = 1.0 / (ROPE_THETA ** (jnp.arange(0, dim, 2, dtype=F32) / dim))
    ang = jnp.arange(seq, dtype=F32)[:, None] * inv_freq[None, :]
    ang = jnp.concatenate([ang, ang], axis=-1)
    return jnp.cos(ang), jnp.sin(ang)


def _tile(n, want):
    want = min(n, want)
    assert n % want == 0, (n, want)
    return want


def kernel(x, attn_norm_w, w_in, lambda_q1, lambda_k1, lambda_q2, lambda_k2, subln_w, conv_w,
           w_proj_attn, w_proj_conv, w_out, ffn_norm_w, w_query, sub_keys, expert_u, expert_v,
           final_norm_w):
    batch, seq, d = x.shape
    depth = w_in.shape[0]
    t = batch * seq
    head_dim = lambda_q1.shape[-1]
    ev = subln_w.shape[-1]
    attn_w = w_proj_attn.shape[1]
    conv_wd = conv_w.shape[-1]
    n_heads = attn_w // ev
    qk_w = n_heads * 2 * head_dim
    assert 2 * head_dim == LANES and ev == LANES
    assert w_in.shape[-1] == 2 * qk_w + attn_w + 3 * conv_wd + 2 * d
    peer_heads, _, n_keys, half_dim = sub_keys.shape[1:]
    assert n_keys == LANES and half_dim == LANES
    n_experts = expert_u.shape[1]

    cos, sin = _rope_tables(seq, head_dim)
    sin = jnp.concatenate([-sin[:, :head_dim // 2], sin[:, head_dim // 2:]], axis=-1)
    cos = jnp.concatenate([cos, cos], axis=-1)
    sin = jnp.concatenate([sin, sin], axis=-1)

    tm = _tile(seq, 256)
    tq_attn = _tile(seq, 512)
    tk_attn = _tile(seq, 1024)
    tl = _tile(t, LANES)
    tq_peer = _tile(t, 512)
    te = _tile(n_experts, 1024)

    h = x.reshape(t, d)
    for l in range(depth):
        lambda_init = 0.8 - 0.6 * math.exp(-0.3 * l)
        offs = [0]
        for sz in (qk_w, qk_w, attn_w, conv_wd, conv_wd, conv_wd, d, d):
            offs.append(offs[-1] + sz)
        wl = w_in[l]
        wq, wk, wv, wb, wc, wx, wga, wgb = [wl[:, offs[n]:offs[n + 1]] for n in range(8)]
        ws = tuple(w.astype(BF16) for w in (wq, wk, wv, wb, wc, wx, wga, wgb))

        q, k, v, bg, u, sga, sgb = _inproj(
            h, attn_norm_w[l][None], cos, sin, ws, seq=seq, tm=tm,
            q_scale=1.0 / math.sqrt(head_dim), half=head_dim // 2)
        attn = _attention(
            q, k, v, lambda_q1[l][None], lambda_k1[l][None], lambda_q2[l][None],
            lambda_k2[l][None], subln_w[l][None], batch=batch, seq=seq, n_heads=n_heads,
            tq=tq_attn, tk=tk_attn, lambda_init=lambda_init)
        sk = sub_keys[l].reshape(peer_heads * 2, n_keys, half_dim).astype(BF16)
        h, hn, sc = _merge(
            attn, bg, u, sga, sgb, h, conv_w[l], w_proj_attn[l].astype(BF16),
            w_proj_conv[l].astype(BF16), w_out[l].astype(BF16), ffn_norm_w[l][None],
            w_query[l].astype(BF16), sk, seq=seq, tm=tm)
        c1, r2, a, b = _route(sc, tl=tl)
        vt_tiles = expert_v[l].astype(BF16).reshape(n_experts // te, te, d).transpose(0, 2, 1)
        h = _experts(
            hn, expert_u[l].astype(BF16), vt_tiles, c1, a, r2, b, h,
            final_norm_w[None], tq=tq_peer, te=te, lane_chunk=_tile(tq_peer, 256),
            final_norm=(l == depth - 1))
    return h.reshape(batch, seq, d)
```

```python
import functools
import math

import jax
import jax.numpy as jnp
from jax import lax
from jax.experimental import pallas as pl
from jax.experimental.pallas import tpu as pltpu

NORM_EPS = 1e-6
ROPE_THETA = 10000.0
PEER_TOPK = 16
LANES = 128
SUBLANES = 8
VMEM_LIMIT_BYTES = 56 * 1024 * 1024

F32 = jnp.float32
BF16 = jnp.bfloat16
NEG_INF = float("-inf")


def _nt_dot(a, b):
    return lax.dot_general(a, b, (((1,), (1,)), ((), ())), preferred_element_type=F32)


def _dot(a, b):
    return jnp.dot(a, b, preferred_element_type=F32)


def _const_spec(shape):
    nd = len(shape)
    return pl.BlockSpec(shape, lambda *_: (0,) * nd)


def _inproj_kernel(x_ref, nw_ref, cos_ref, sin_ref, wq_ref, wk_ref, wv_ref,
                   wb_ref, wc_ref, wx_ref, wga_ref, wgb_ref,
                   q_ref, k_ref, v_ref, bg_ref, u_ref, sga_ref, sgb_ref, *, q_scale, half):
    x = x_ref[...]
    ms = jnp.mean(x * x, axis=-1, keepdims=True)
    xn = (x * lax.rsqrt(ms + NORM_EPS) * nw_ref[...]).astype(BF16)
    cos = cos_ref[...]
    sin = sin_ref[...]
    lane = lax.broadcasted_iota(jnp.int32, cos.shape, 1)
    low = (lane % (2 * half)) < half

    def rope(w_ref, o_ref, scale):
        t = _dot(xn, w_ref[...])
        for blk in range(t.shape[1] // LANES):
            sl = slice(blk * LANES, (blk + 1) * LANES)
            tb = t[:, sl]
            partner = jnp.where(low, pltpu.roll(tb, LANES - half, axis=1),
                                pltpu.roll(tb, half, axis=1))
            r = tb * cos + partner * sin
            if scale != 1.0:
                r = r * scale
            o_ref[:, sl] = r.astype(o_ref.dtype)

    rope(wq_ref, q_ref, q_scale)
    rope(wk_ref, k_ref, 1.0)
    v_ref[...] = _dot(xn, wv_ref[...]).astype(v_ref.dtype)
    bg_ref[...] = _dot(xn, wb_ref[...])
    u_ref[...] = _dot(xn, wc_ref[...]) * _dot(xn, wx_ref[...])
    sga_ref[...] = jax.nn.sigmoid(_dot(xn, wga_ref[...]))
    sgb_ref[...] = jax.nn.sigmoid(_dot(xn, wgb_ref[...]))


def _inproj(xf, norm_w, cos, sin, ws, *, seq, tm, q_scale, half):
    t, d = xf.shape
    wq, wk, wv, wb, wc, wx, wga, wgb = ws
    n_seq_tiles = seq // tm
    row = lambda c: pl.BlockSpec((tm, c), lambda i: (i, 0))
    out_shapes = (
        jax.ShapeDtypeStruct((t, wq.shape[1]), BF16),
        jax.ShapeDtypeStruct((t, wk.shape[1]), BF16),
        jax.ShapeDtypeStruct((t, wv.shape[1]), BF16),
        jax.ShapeDtypeStruct((t, wb.shape[1]), F32),
        jax.ShapeDtypeStruct((t, wc.shape[1]), F32),
        jax.ShapeDtypeStruct((t, wga.shape[1]), F32),
        jax.ShapeDtypeStruct((t, wgb.shape[1]), F32),
    )
    return pl.pallas_call(
        functools.partial(_inproj_kernel, q_scale=q_scale, half=half),
        grid=(t // tm,),
        in_specs=[row(d), _const_spec((1, d)),
                  pl.BlockSpec((tm, LANES), lambda i: (i % n_seq_tiles, 0)),
                  pl.BlockSpec((tm, LANES), lambda i: (i % n_seq_tiles, 0))]
                 + [_const_spec(w.shape) for w in ws],
        out_specs=[row(s.shape[1]) for s in out_shapes],
        out_shape=out_shapes,
        compiler_params=pltpu.CompilerParams(dimension_semantics=("arbitrary",),
                                             vmem_limit_bytes=VMEM_LIMIT_BYTES),
        name="inproj",
    )(xf, norm_w, cos, sin, *ws)


def _attn_kernel(lq1_ref, lk1_ref, lq2_ref, lk2_ref, sw_ref, q_ref, k_ref, v_ref, o_ref,
                 *, tk, head_dim, lambda_init):
    tq = q_ref.shape[0]
    seq = k_ref.shape[0]
    q = q_ref[...]
    lane = lax.broadcasted_iota(jnp.int32, q.shape, 1)
    zero = jnp.zeros_like(q)
    q1 = jnp.where(lane < head_dim, q, zero)
    q2 = jnp.where(lane >= head_dim, q, zero)
    lam = (jnp.exp(jnp.sum(lq1_ref[...] * lk1_ref[...], axis=-1, keepdims=True))
           - jnp.exp(jnp.sum(lq2_ref[...] * lk2_ref[...], axis=-1, keepdims=True)) + lambda_init)

    def lane_fold(op, acc, s):
        for c in range(s.shape[1] // LANES):
            acc = op(acc, s[:, c * LANES:(c + 1) * LANES])
        return acc

    def softmax_values(qm):
        parts = []
        for j in range(seq // tk):
            s = _nt_dot(qm, k_ref[j * tk:(j + 1) * tk, :])
            mc = jnp.max(lane_fold(jnp.maximum, s[:, :LANES], s[:, LANES:]), axis=-1, keepdims=True)
            p = jnp.exp(s - mc)
            lsum = lane_fold(jnp.add, p[:, :LANES], p[:, LANES:])
            parts.append((mc, lsum, _dot(p.astype(BF16), v_ref[j * tk:(j + 1) * tk, :])))
        m = parts[0][0]
        for mc, _, _ in parts[1:]:
            m = jnp.maximum(m, mc)
        lsum = acc = None
        for mc, lc, oc in parts:
            w = jnp.exp(mc - m)
            lsum = w * lc if lsum is None else lsum + w * lc
            acc = w * oc if acc is None else acc + w * oc
        return jnp.sum(lsum, axis=-1, keepdims=True), acc

    l1, a1 = softmax_values(q1)
    l2, a2 = softmax_values(q2)
    o = a1 / l1 - lam * (a2 / l2)
    ms = jnp.mean(o * o, axis=-1, keepdims=True)
    o = o * lax.rsqrt(ms + NORM_EPS) * sw_ref[...]
    o_ref[...] = (o * (1.0 - lambda_init)).astype(o_ref.dtype)


def _attention(q, k, v, lq1, lk1, lq2, lk2, subln_w, *, batch, seq, n_heads, tq, tk, lambda_init):
    t = q.shape[0]
    ev = v.shape[1] // n_heads
    head_dim = lq1.shape[-1]
    nq = seq // tq
    small = lambda a: _const_spec(a.shape)
    return pl.pallas_call(
        functools.partial(_attn_kernel, tk=tk, head_dim=head_dim,
                          lambda_init=lambda_init),
        grid=(batch, n_heads, nq),
        in_specs=[small(lq1), small(lk1), small(lq2), small(lk2), small(subln_w),
                  pl.BlockSpec((tq, 2 * head_dim), lambda b, h, i: (b * nq + i, h)),
                  pl.BlockSpec((seq, 2 * head_dim), lambda b, h, i: (b, h)),
                  pl.BlockSpec((seq, ev), lambda b, h, i: (b, h))],
        out_specs=pl.BlockSpec((tq, ev), lambda b, h, i: (b * nq + i, h)),
        out_shape=jax.ShapeDtypeStruct((t, v.shape[1]), BF16),
        compiler_params=pltpu.CompilerParams(
            dimension_semantics=("arbitrary", "arbitrary", "arbitrary"),
            vmem_limit_bytes=VMEM_LIMIT_BYTES),
        name="diff_attn",
    )(lq1, lk1, lq2, lk2, subln_w, q, k, v)


def _merge_kernel(attn_ref, bg_ref, u_ref, up_ref, un_ref, sga_ref, sgb_ref, x_ref,
                  cw_ref, wpa_ref, wpb_ref, wo_ref, fw_ref, wqy_ref, sk_ref,
                  h_ref, hn_ref, sc_ref, *, n_seq_tiles):
    i = pl.program_id(0)
    tm = u_ref.shape[0]
    u = u_ref[...]
    row = lax.broadcasted_iota(jnp.int32, u.shape, 0)
    first = (i % n_seq_tiles) == 0
    last = (i % n_seq_tiles) == n_seq_tiles - 1
    prev_row = jnp.where(first, 0.0, up_ref[SUBLANES - 1:SUBLANES, :])
    next_row = jnp.where(last, 0.0, un_ref[0:1, :])
    u_prev = jnp.where(row == 0, prev_row, pltpu.roll(u, 1, axis=0))
    u_next = jnp.where(row == tm - 1, next_row, pltpu.roll(u, tm - 1, axis=0))
    cw = cw_ref[...]
    conv = cw[0:1, :] * u_prev + cw[1:2, :] * u + cw[2:3, :] * u_next
    y_conv = _dot((bg_ref[...] * conv).astype(BF16), wpb_ref[...])
    y_attn = _dot(attn_ref[...], wpa_ref[...])
    merged = sga_ref[...] * y_attn + sgb_ref[...] * y_conv
    h = x_ref[...] + _dot(merged.astype(BF16), wo_ref[...])
    h_ref[...] = h
    ms = jnp.mean(h * h, axis=-1, keepdims=True)
    hn_f32 = h * lax.rsqrt(ms + NORM_EPS) * fw_ref[...]
    hn = hn_f32.astype(BF16)
    hn_ref[...] = hn_f32.T.astype(BF16)
    qp = _dot(hn, wqy_ref[...]).astype(BF16)
    for g in range(sk_ref.shape[0]):
        sc_ref[g] = _nt_dot(sk_ref[g], qp[:, g * LANES:(g + 1) * LANES])


def _merge(attn, bg, u, sga, sgb, xf, conv_w, wpa, wpb, wo, ffn_w, wqy, sk, *, seq, tm):
    t, d = xf.shape
    cw = u.shape[1]
    n_tiles = t // tm
    n_seq_tiles = seq // tm
    rb = tm // SUBLANES
    n_rb = t // SUBLANES
    row = lambda c: pl.BlockSpec((tm, c), lambda i: (i, 0))
    n_groups, n_keys, _ = sk.shape
    out_shapes = (jax.ShapeDtypeStruct((t, d), F32),
                  jax.ShapeDtypeStruct((d, t), BF16),
                  jax.ShapeDtypeStruct((n_groups, n_keys, t), F32))
    return pl.pallas_call(
        functools.partial(_merge_kernel, n_seq_tiles=n_seq_tiles),
        grid=(n_tiles,),
        in_specs=[row(attn.shape[1]), row(cw), row(cw),
                  pl.BlockSpec((SUBLANES, cw), lambda i: (jnp.maximum(i * rb - 1, 0), 0)),
                  pl.BlockSpec((SUBLANES, cw), lambda i: (jnp.minimum((i + 1) * rb, n_rb - 1), 0)),
                  row(d), row(d), row(d),
                  _const_spec(conv_w.shape), _const_spec(wpa.shape), _const_spec(wpb.shape),
                  _const_spec(wo.shape), _const_spec(ffn_w.shape), _const_spec(wqy.shape),
                  _const_spec(sk.shape)],
        out_specs=[row(d), pl.BlockSpec((d, tm), lambda i: (0, i)),
                   pl.BlockSpec((n_groups, n_keys, tm), lambda i: (0, 0, i))],
        out_shape=out_shapes,
        compiler_params=pltpu.CompilerParams(dimension_semantics=("arbitrary",),
                                             vmem_limit_bytes=VMEM_LIMIT_BYTES),
        name="merge",
    )(attn, bg, u, u, u, sga, sgb, xf, conv_w, wpa, wpb, wo, ffn_w, wqy, sk)


def _top16(s):
    n = s.shape[0]
    k = PEER_TOPK
    idx = lax.broadcasted_iota(jnp.int32, s.shape, 0)
    rank = jnp.full(s.shape, k, jnp.int32)
    cur = s
    vals = []
    for r in range(k):
        mx = jnp.max(cur, axis=0, keepdims=True)
        first = jnp.min(jnp.where(cur == mx, idx, n), axis=0, keepdims=True)
        hit = idx == first
        rank = jnp.where(hit, r, rank)
        cur = jnp.where(hit, NEG_INF, cur)
        vals.append(mx)
    return rank, vals, jnp.zeros((1, s.shape[1]), jnp.bool_)


def _batcher_sort_network(lo, hi):
    def merge(lo, hi, r):
        step = 2 * r
        if step < hi - lo:
            yield from merge(lo, hi, step)
            yield from merge(lo + r, hi, step)
            yield from ((i, i + r) for i in range(lo + r, hi - r, step))
        else:
            yield (lo, lo + r)
    if hi - lo >= 1:
        mid = lo + (hi - lo) // 2
        yield from _batcher_sort_network(lo, mid)
        yield from _batcher_sort_network(mid + 1, hi)
        yield from merge(lo, hi, 1)


def _bitonic_merge_network(n):
    d = n // 2
    while d >= 1:
        yield from ((i, i + d) for i in range(n) if not i & d)
        d //= 2


_SORT16 = tuple(_batcher_sort_network(0, PEER_TOPK - 1))
_MERGE16 = tuple(_bitonic_merge_network(PEER_TOPK))


def _sublane_pieces(s):
    return [s[i * SUBLANES:(i + 1) * SUBLANES] for i in range(s.shape[0] // SUBLANES)]


def _sorted_top16(pieces):
    k = PEER_TOPK
    n = len(pieces)
    assert k <= 2 * n and n <= k

    def exchange(rows, network):
        for i, j in network:
            rows[i], rows[j] = jnp.maximum(rows[i], rows[j]), jnp.minimum(rows[i], rows[j])

    rows = list(pieces)
    exchange(rows, [(i, j) for i, j in _SORT16 if j < n])
    shift = 1
    while shift < SUBLANES:
        other = [pltpu.roll(x, shift, axis=0) for x in rows]
        merged = []
        for i in range(k):
            mine = rows[i] if i < len(rows) else None
            theirs = other[k - 1 - i] if k - 1 - i < len(other) else None
            merged.append(mine if theirs is None else theirs if mine is None
                          else jnp.maximum(mine, theirs))
        rows = merged
        exchange(rows, _MERGE16)
        shift *= 2
    return rows


def _top16_network(pieces):
    k = PEER_TOPK
    assert len(pieces) == k
    rows = _sorted_top16(pieces)
    dup = rows[0] == rows[1]
    for r in range(1, k - 1):
        dup = jnp.logical_or(dup, rows[r] == rows[r + 1])
    n_ge = jnp.zeros(pieces[0].shape, F32)
    for p in pieces:
        n_ge = n_ge + jnp.where(p >= rows[k - 1], 1.0, 0.0)
    n_ge = jnp.sum(n_ge, axis=0, keepdims=True)
    return rows, jnp.logical_or(dup[0:1], n_ge != float(k))


def _rank_by_count(pieces, rows):
    out = []
    for p in pieces:
        rank = jnp.zeros(p.shape, F32)
        for r, v in enumerate(rows):
            rank = jnp.where(v > p, float(r + 1), rank)
        out.append(rank)
    return jnp.concatenate(out, axis=0)


def _route_kernel(sc_ref, c1_ref, r2_ref, a_ref, b_ref):
    tied = _route_pass(sc_ref, c1_ref, r2_ref, a_ref, b_ref, break_ties=False)

    @pl.when(jnp.max(tied.astype(F32)) > 0.0)
    def _():
        _route_pass(sc_ref, c1_ref, r2_ref, a_ref, b_ref, break_ties=True)


def _route_pass(sc_ref, c1_ref, r2_ref, a_ref, b_ref, *, break_ties):
    tl = sc_ref.shape[-1]
    k = PEER_TOPK
    i16 = lax.broadcasted_iota(jnp.int32, (k, tl), 0)
    i8 = lax.broadcasted_iota(jnp.int32, (SUBLANES, tl), 0)
    order = [i16] + [r1 * k + i8 for r1 in range(1, 8)] + [(8 + i8) * k]
    order = jnp.concatenate(order, axis=0)
    valid = [i16 >= 0] + [i8 < (k // (r1 + 1)) for r1 in range(1, 8)] + [i8 >= 0]
    valid = jnp.concatenate(valid, axis=0)
    big = 2 * k * k

    def head(h, tied):
        s1 = sc_ref[2 * h]
        s2 = sc_ref[2 * h + 1]
        if break_ties:
            rank1, v1, tied1 = _top16(s1)
            rank2, v2, tied2 = _top16(s2)
            rank2 = rank2.astype(F32)
        else:
            p1, p2 = _sublane_pieces(s1), _sublane_pieces(s2)
            rows1, tied1 = _top16_network(p1)
            rows2, tied2 = _top16_network(p2)
            v1 = [x[0:1] for x in rows1]
            v2 = [x[0:1] for x in rows2]
            rank2 = _rank_by_count(p2, rows2)
        v2a = jnp.concatenate(v2, axis=0)
        v1b = jnp.concatenate(v1[8:], axis=0)
        e1 = [jnp.exp(v - v1[0]) for v in v1]
        e2a = jnp.exp(v2a - v2[0])
        e1b = jnp.concatenate(e1[8:], axis=0)
        cand = ([v1[0] + v2a] + [v1[r1] + v2a[:SUBLANES] for r1 in range(1, 8)]
                + [v1b + v2[0]])
        cand = jnp.where(valid, jnp.concatenate(cand, axis=0), NEG_INF)
        prod = ([e1[0] * e2a] + [e1[r1] * e2a[:SUBLANES] for r1 in range(1, 8)]
                + [e1b * e2a[0:1]])
        prod = jnp.concatenate(prod, axis=0)
        if break_ties:
            cur = cand
            for _ in range(k):
                mx = jnp.max(cur, axis=0, keepdims=True)
                first = jnp.min(jnp.where(cur == mx, order, big), axis=0, keepdims=True)
                cur = jnp.where(order == first, NEG_INF, cur)
            taken = jnp.where(cur != cand, 1.0, 0.0)
        else:
            cand_pieces = _sublane_pieces(cand)
            kth = _sorted_top16(cand_pieces)[k - 1]
            taken = jnp.concatenate([jnp.where(c >= kth, 1.0, 0.0) for c in cand_pieces], axis=0)
        z = jnp.sum(taken * prod, axis=0, keepdims=True)
        counts = [jnp.sum(taken[0:k], axis=0, keepdims=True)]
        for r1 in range(1, 8):
            lo = k + (r1 - 1) * SUBLANES
            counts.append(jnp.sum(taken[lo:lo + SUBLANES], axis=0, keepdims=True))
        lo = k + 7 * SUBLANES
        counts += [taken[lo + j:lo + j + 1] for j in range(8)]
        if break_ties:
            c1 = jnp.zeros(s1.shape, F32)
            for r in range(k):
                c1 = jnp.where(rank1 == r, counts[r], c1)
        else:
            wide = [jnp.broadcast_to(c, (SUBLANES, tl)) for c in counts]
            c1 = []
            for p in p1:
                c = jnp.zeros(p.shape, F32)
                for r in range(k):
                    c = jnp.where(p == rows1[r], wide[r], c)
                c1.append(c)
            c1 = jnp.concatenate(c1, axis=0)
        c1_ref[h] = c1
        r2_ref[h] = rank2.astype(BF16)
        a_ref[h] = jnp.exp(s1 - v1[0]) / z
        b_ref[h] = jnp.exp(s2 - v2[0]).astype(BF16)
        n_taken = counts[0]
        for c in counts[1:]:
            n_taken = n_taken + c
        tied_now = jnp.logical_or(jnp.logical_or(tied1, tied2), n_taken != float(k))
        return jnp.maximum(tied, jnp.where(tied_now, 1.0, 0.0))

    return lax.fori_loop(0, sc_ref.shape[0] // 2, head, jnp.zeros((1, tl), F32),
                         unroll=1 if break_ties else 4)


def _route(sc, *, tl):
    n_groups, n_keys, t = sc.shape
    nh = n_groups // 2
    words = jax.ShapeDtypeStruct((nh, n_keys, t), F32)
    halfs = jax.ShapeDtypeStruct((nh, n_keys, t), BF16)
    spec = pl.BlockSpec((nh, n_keys, tl), lambda i: (0, 0, i))
    return pl.pallas_call(
        _route_kernel,
        grid=(t // tl,),
        in_specs=[pl.BlockSpec((n_groups, n_keys, tl), lambda i: (0, 0, i))],
        out_specs=[spec] * 4,
        out_shape=(words, halfs, words, halfs),
        compiler_params=pltpu.CompilerParams(dimension_semantics=("arbitrary",),
                                             vmem_limit_bytes=VMEM_LIMIT_BYTES),
        name="peer_route",
    )(sc)


def _bf16_rows(row, n_rows):
    packed_rows = 2 * SUBLANES
    tile = jnp.broadcast_to(row, (packed_rows, row.shape[1])).astype(BF16)
    return jnp.concatenate([tile] * (n_rows // packed_rows), axis=0)


def _expert_kernel(hn_ref, u_ref, vt_ref, c1_ref, a_ref, r2_ref, b_ref, h_ref, fw_ref,
                   o_ref, acc_ref, wt_ref, ht0_ref, ht1_ref, *, n_j, final_norm, lane_chunk):
    s = pl.program_id(0)
    n_heads, n_keys, tq = r2_ref.shape
    rows = c1_ref.shape[1]
    j_prev = jnp.maximum(s - 1, 0) % n_j

    @pl.when(s == 0)
    def _():
        ht1_ref[...] = jnp.zeros_like(ht1_ref)

    @pl.when(j_prev == 0)
    def _():
        acc_ref[...] = jnp.zeros_like(acc_ref)

    def step(ht_w_ref, ht_r_ref):
        for t0 in range(0, tq, lane_chunk):
            tsl = slice(t0, t0 + lane_chunk)
            ht_w_ref[:, tsl] = _dot(u_ref[...], hn_ref[:, tsl])
            for r in range(rows):
                x = ht_r_ref[r * n_keys:(r + 1) * n_keys, tsl]
                act = (x * (0.5 + 0.5 * lax.erf(x * (1.0 / math.sqrt(2.0))))).astype(BF16)
                g = None
                for hd in range(n_heads):
                    sel = r2_ref[hd, :, tsl] < _bf16_rows(c1_ref[hd, r:r + 1, tsl], n_keys)
                    term = (jnp.where(sel, b_ref[hd, :, tsl], jnp.zeros((), BF16))
                            * _bf16_rows(a_ref[hd, r:r + 1, tsl], n_keys))
                    g = term if g is None else g + term
                wt_ref[r * n_keys:(r + 1) * n_keys, tsl] = g * act
            acc_ref[:, tsl] += _dot(vt_ref[...], wt_ref[:, tsl])

    @pl.when(s % 2 == 0)
    def _():
        step(ht0_ref, ht1_ref)

    @pl.when(s % 2 == 1)
    def _():
        step(ht1_ref, ht0_ref)

    @pl.when(jnp.logical_and(j_prev == n_j - 1, s > 0))
    def _():
        h = h_ref[...] + acc_ref[...].T
        if final_norm:
            ms = jnp.mean(h * h, axis=-1, keepdims=True)
            h = h * lax.rsqrt(ms + NORM_EPS) * fw_ref[...]
        o_ref[...] = h


def _experts(hn_t, u_bf, vt_bf, c1, a, r2, b, h, final_w, *, tq, te, lane_chunk, final_norm):
    d, t = hn_t.shape
    n_experts = u_bf.shape[0]
    nh, n_keys, _ = r2.shape
    rows = te // n_keys
    n_j = n_experts // te
    n_tiles = (t // tq) * n_j
    cur = lambda s: jnp.minimum(s, n_tiles - 1)
    prev = lambda s: jnp.maximum(s - 1, 0)
    return pl.pallas_call(
        functools.partial(_expert_kernel, n_j=n_j, final_norm=final_norm, lane_chunk=lane_chunk),
        grid=(n_tiles + 1,),
        in_specs=[pl.BlockSpec((d, tq), lambda s: (0, cur(s) // n_j)),
                  pl.BlockSpec((te, d), lambda s: (cur(s) % n_j, 0)),
                  pl.BlockSpec((None, d, te), lambda s: (prev(s) % n_j, 0, 0)),
                  pl.BlockSpec((nh, rows, tq), lambda s: (0, prev(s) % n_j, prev(s) // n_j)),
                  pl.BlockSpec((nh, rows, tq), lambda s: (0, prev(s) % n_j, prev(s) // n_j)),
                  pl.BlockSpec((nh, n_keys, tq), lambda s: (0, 0, prev(s) // n_j)),
                  pl.BlockSpec((nh, n_keys, tq), lambda s: (0, 0, prev(s) // n_j)),
                  pl.BlockSpec((tq, d), lambda s: (prev(s) // n_j, 0)),
                  _const_spec(final_w.shape)],
        out_specs=pl.BlockSpec((tq, d), lambda s: (prev(s) // n_j, 0)),
        out_shape=jax.ShapeDtypeStruct((t, d), F32),
        scratch_shapes=[pltpu.VMEM((d, tq), F32), pltpu.VMEM((te, tq), BF16),
                        pltpu.VMEM((te, tq), F32), pltpu.VMEM((te, tq), F32)],
        compiler_params=pltpu.CompilerParams(dimension_semantics=("arbitrary",),
                                             vmem_limit_bytes=VMEM_LIMIT_BYTES),
        name="peer_experts",
    )(hn_t, u_bf, vt_bf, c1, a, r2, b, h, final_w)


def _rope_tables(seq, dim):
    inv_freq = 1.0 / (ROPE_THETA ** (jnp.arange(0, dim, 2, dtype=F32) / dim))
    ang = jnp.arange(seq, dtype=F32)[:, None] * inv_freq[None, :]
    ang = jnp.concatenate([ang, ang], axis=-1)
    return jnp.cos(ang), jnp.sin(ang)


def _tile(n, want):
    want = min(n, want)
    assert n % want == 0, (n, want)
    return want


def kernel(x, attn_norm_w, w_in, lambda_q1, lambda_k1, lambda_q2, lambda_k2, subln_w, conv_w,
           w_proj_attn, w_proj_conv, w_out, ffn_norm_w, w_query, sub_keys, expert_u, expert_v,
           final_norm_w):
    batch, seq, d = x.shape
    depth = w_in.shape[0]
    t = batch * seq
    head_dim = lambda_q1.shape[-1]
    ev = subln_w.shape[-1]
    attn_w = w_proj_attn.shape[1]
    conv_wd = conv_w.shape[-1]
    n_heads = attn_w // ev
    qk_w = n_heads * 2 * head_dim
    assert 2 * head_dim == LANES and ev == LANES
    assert w_in.shape[-1] == 2 * qk_w + attn_w + 3 * conv_wd + 2 * d
    peer_heads, _, n_keys, half_dim = sub_keys.shape[1:]
    assert n_keys == LANES and half_dim == LANES
    n_experts = expert_u.shape[1]

    cos, sin = _rope_tables(seq, head_dim)
    sin = jnp.concatenate([-sin[:, :head_dim // 2], sin[:, head_dim // 2:]], axis=-1)
    cos = jnp.concatenate([cos, cos], axis=-1)
    sin = jnp.concatenate([sin, sin], axis=-1)

    tm = _tile(seq, 256)
    tq_attn = _tile(seq, 1024)
    tk_attn = _tile(seq, 1024)
    tl = _tile(t, LANES)
    tq_peer = _tile(t, 512)
    te = _tile(n_experts, 1024)

    h = x.reshape(t, d)
    for l in range(depth):
        lambda_init = 0.8 - 0.6 * math.exp(-0.3 * l)
        offs = [0]
        for sz in (qk_w, qk_w, attn_w, conv_wd, conv_wd, conv_wd, d, d):
            offs.append(offs[-1] + sz)
        wl = w_in[l]
        wq, wk, wv, wb, wc, wx, wga, wgb = [wl[:, offs[n]:offs[n + 1]] for n in range(8)]
        ws = tuple(w.astype(BF16) for w in (wq, wk, wv, wb, wc, wx, wga, wgb))

        q, k, v, bg, u, sga, sgb = _inproj(
            h, attn_norm_w[l][None], cos, sin, ws, seq=seq, tm=tm,
            q_scale=1.0 / math.sqrt(head_dim), half=head_dim // 2)
        attn = _attention(
            q, k, v, lambda_q1[l][None], lambda_k1[l][None], lambda_q2[l][None],
            lambda_k2[l][None], subln_w[l][None], batch=batch, seq=seq, n_heads=n_heads,
            tq=tq_attn, tk=tk_attn, lambda_init=lambda_init)
        sk = sub_keys[l].reshape(peer_heads * 2, n_keys, half_dim).astype(BF16)
        h, hn, sc = _merge(
            attn, bg, u, sga, sgb, h, conv_w[l], w_proj_attn[l].astype(BF16),
            w_proj_conv[l].astype(BF16), w_out[l].astype(BF16), ffn_norm_w[l][None],
            w_query[l].astype(BF16), sk, seq=seq, tm=tm)
        c1, r2, a, b = _route(sc, tl=tl)
        vt_tiles = expert_v[l].astype(BF16).reshape(n_experts // te, te, d).transpose(0, 2, 1)
        h = _experts(
            hn, expert_u[l].astype(BF16), vt_tiles, c1, a, r2, b, h,
            final_norm_w[None], tq=tq_peer, te=te, lane_chunk=_tile(tq_peer, 256),
            final_norm=(l == depth - 1))
    return h.reshape(batch, seq, d)
```

```python
import functools
import math

import jax
import jax.numpy as jnp
from jax import lax
from jax.experimental import pallas as pl
from jax.experimental.pallas import tpu as pltpu

NORM_EPS = 1e-6
ROPE_THETA = 10000.0
PEER_TOPK = 16
LANES = 128
SUBLANES = 8
VMEM_LIMIT_BYTES = 56 * 1024 * 1024

F32 = jnp.float32
BF16 = jnp.bfloat16
NEG_INF = float("-inf")


def _nt_dot(a, b):
    return lax.dot_general(a, b, (((1,), (1,)), ((), ())), preferred_element_type=F32)


def _dot(a, b):
    return jnp.dot(a, b, preferred_element_type=F32)


def _const_spec(shape):
    nd = len(shape)
    return pl.BlockSpec(shape, lambda *_: (0,) * nd)


def _inproj_kernel(x_ref, nw_ref, cos_ref, sin_ref, wq_ref, wk_ref, wv_ref,
                   wb_ref, wc_ref, wx_ref, wga_ref, wgb_ref,
                   q_ref, k_ref, v_ref, bg_ref, u_ref, sga_ref, sgb_ref, *, q_scale, half):
    x = x_ref[...]
    ms = jnp.mean(x * x, axis=-1, keepdims=True)
    xn = (x * lax.rsqrt(ms + NORM_EPS) * nw_ref[...]).astype(BF16)
    cos = cos_ref[...]
    sin = sin_ref[...]
    lane = lax.broadcasted_iota(jnp.int32, cos.shape, 1)
    low = (lane % (2 * half)) < half

    def rope(w_ref, o_ref, scale):
        t = _dot(xn, w_ref[...])
        for blk in range(t.shape[1] // LANES):
            sl = slice(blk * LANES, (blk + 1) * LANES)
            tb = t[:, sl]
            partner = jnp.where(low, pltpu.roll(tb, LANES - half, axis=1),
                                pltpu.roll(tb, half, axis=1))
            r = tb * cos + partner * sin
            if scale != 1.0:
                r = r * scale
            o_ref[:, sl] = r.astype(o_ref.dtype)

    rope(wq_ref, q_ref, q_scale)
    rope(wk_ref, k_ref, 1.0)
    v_ref[...] = _dot(xn, wv_ref[...]).astype(v_ref.dtype)
    bg_ref[...] = _dot(xn, wb_ref[...])
    u_ref[...] = _dot(xn, wc_ref[...]) * _dot(xn, wx_ref[...])
    sga_ref[...] = jax.nn.sigmoid(_dot(xn, wga_ref[...]))
    sgb_ref[...] = jax.nn.sigmoid(_dot(xn, wgb_ref[...]))


def _inproj(xf, norm_w, cos, sin, ws, *, seq, tm, q_scale, half):
    t, d = xf.shape
    wq, wk, wv, wb, wc, wx, wga, wgb = ws
    n_seq_tiles = seq // tm
    row = lambda c: pl.BlockSpec((tm, c), lambda i: (i, 0))
    out_shapes = (
        jax.ShapeDtypeStruct((t, wq.shape[1]), BF16),
        jax.ShapeDtypeStruct((t, wk.shape[1]), BF16),
        jax.ShapeDtypeStruct((t, wv.shape[1]), BF16),
        jax.ShapeDtypeStruct((t, wb.shape[1]), F32),
        jax.ShapeDtypeStruct((t, wc.shape[1]), F32),
        jax.ShapeDtypeStruct((t, wga.shape[1]), F32),
        jax.ShapeDtypeStruct((t, wgb.shape[1]), F32),
    )
    return pl.pallas_call(
        functools.partial(_inproj_kernel, q_scale=q_scale, half=half),
        grid=(t // tm,),
        in_specs=[row(d), _const_spec((1, d)),
                  pl.BlockSpec((tm, LANES), lambda i: (i % n_seq_tiles, 0)),
                  pl.BlockSpec((tm, LANES), lambda i: (i % n_seq_tiles, 0))]
                 + [_const_spec(w.shape) for w in ws],
        out_specs=[row(s.shape[1]) for s in out_shapes],
        out_shape=out_shapes,
        compiler_params=pltpu.CompilerParams(dimension_semantics=("arbitrary",),
                                             vmem_limit_bytes=VMEM_LIMIT_BYTES),
        name="inproj",
    )(xf, norm_w, cos, sin, *ws)


def _attn_kernel(lq1_ref, lk1_ref, lq2_ref, lk2_ref, sw_ref, q_ref, k_ref, v_ref, o_ref,
                 *, tk, head_dim, lambda_init):
    tq = q_ref.shape[0]
    seq = k_ref.shape[0]
    q = q_ref[...]
    lane = lax.broadcasted_iota(jnp.int32, q.shape, 1)
    zero = jnp.zeros_like(q)
    q1 = jnp.where(lane < head_dim, q, zero)
    q2 = jnp.where(lane >= head_dim, q, zero)
    lam = (jnp.exp(jnp.sum(lq1_ref[...] * lk1_ref[...], axis=-1, keepdims=True))
           - jnp.exp(jnp.sum(lq2_ref[...] * lk2_ref[...], axis=-1, keepdims=True)) + lambda_init)

    def lane_fold(op, acc, s):
        for c in range(s.shape[1] // LANES):
            acc = op(acc, s[:, c * LANES:(c + 1) * LANES])
        return acc

    def softmax_values(qm):
        parts = []
        for j in range(seq // tk):
            s = _nt_dot(qm, k_ref[j * tk:(j + 1) * tk, :])
            mc = jnp.max(lane_fold(jnp.maximum, s[:, :LANES], s[:, LANES:]), axis=-1, keepdims=True)
            p = jnp.exp(s - mc)
            lsum = lane_fold(jnp.add, p[:, :LANES], p[:, LANES:])
            parts.append((mc, lsum, _dot(p.astype(BF16), v_ref[j * tk:(j + 1) * tk, :])))
        m = parts[0][0]
        for mc, _, _ in parts[1:]:
            m = jnp.maximum(m, mc)
        lsum = acc = None
        for mc, lc, oc in parts:
            w = jnp.exp(mc - m)
            lsum = w * lc if lsum is None else lsum + w * lc
            acc = w * oc if acc is None else acc + w * oc
        return jnp.sum(lsum, axis=-1, keepdims=True), acc

    l1, a1 = softmax_values(q1)
    l2, a2 = softmax_values(q2)
    o = a1 / l1 - lam * (a2 / l2)
    ms = jnp.mean(o * o, axis=-1, keepdims=True)
    o = o * lax.rsqrt(ms + NORM_EPS) * sw_ref[...]
    o_ref[...] = (o * (1.0 - lambda_init)).astype(o_ref.dtype)


def _attention(q, k, v, lq1, lk1, lq2, lk2, subln_w, *, batch, seq, n_heads, tq, tk, lambda_init):
    t = q.shape[0]
    ev = v.shape[1] // n_heads
    head_dim = lq1.shape[-1]
    nq = seq // tq
    small = lambda a: _const_spec(a.shape)
    return pl.pallas_call(
        functools.partial(_attn_kernel, tk=tk, head_dim=head_dim,
                          lambda_init=lambda_init),
        grid=(batch, n_heads, nq),
        in_specs=[small(lq1), small(lk1), small(lq2), small(lk2), small(subln_w),
                  pl.BlockSpec((tq, 2 * head_dim), lambda b, h, i: (b * nq + i, h)),
                  pl.BlockSpec((seq, 2 * head_dim), lambda b, h, i: (b, h)),
                  pl.BlockSpec((seq, ev), lambda b, h, i: (b, h))],
        out_specs=pl.BlockSpec((tq, ev), lambda b, h, i: (b * nq + i, h)),
        out_shape=jax.ShapeDtypeStruct((t, v.shape[1]), BF16),
        compiler_params=pltpu.CompilerParams(
            dimension_semantics=("arbitrary", "arbitrary", "arbitrary"),
            vmem_limit_bytes=VMEM_LIMIT_BYTES),
        name="diff_attn",
    )(lq1, lk1, lq2, lk2, subln_w, q, k, v)


def _merge_kernel(attn_ref, bg_ref, u_ref, up_ref, un_ref, sga_ref, sgb_ref, x_ref,
                  cw_ref, wpa_ref, wpb_ref, wo_ref, fw_ref, wqy_ref, sk_ref,
                  h_ref, hn_ref, sc_ref, *, n_seq_tiles):
    i = pl.program_id(0)
    tm = u_ref.shape[0]
    u = u_ref[...]
    row = lax.broadcasted_iota(jnp.int32, u.shape, 0)
    first = (i % n_seq_tiles) == 0
    last = (i % n_seq_tiles) == n_seq_tiles - 1
    prev_row = jnp.where(first, 0.0, up_ref[SUBLANES - 1:SUBLANES, :])
    next_row = jnp.where(last, 0.0, un_ref[0:1, :])
    u_prev = jnp.where(row == 0, prev_row, pltpu.roll(u, 1, axis=0))
    u_next = jnp.where(row == tm - 1, next_row, pltpu.roll(u, tm - 1, axis=0))
    cw = cw_ref[...]
    conv = cw[0:1, :] * u_prev + cw[1:2, :] * u + cw[2:3, :] * u_next
    y_conv = _dot((bg_ref[...] * conv).astype(BF16), wpb_ref[...])
    y_attn = _dot(attn_ref[...], wpa_ref[...])
    merged = sga_ref[...] * y_attn + sgb_ref[...] * y_conv
    h = x_ref[...] + _dot(merged.astype(BF16), wo_ref[...])
    h_ref[...] = h
    ms = jnp.mean(h * h, axis=-1, keepdims=True)
    hn_f32 = h * lax.rsqrt(ms + NORM_EPS) * fw_ref[...]
    hn = hn_f32.astype(BF16)
    hn_ref[...] = hn_f32.T.astype(BF16)
    qp = _dot(hn, wqy_ref[...]).astype(BF16)
    for g in range(sk_ref.shape[0]):
        sc_ref[g] = _nt_dot(sk_ref[g], qp[:, g * LANES:(g + 1) * LANES])


def _merge(attn, bg, u, sga, sgb, xf, conv_w, wpa, wpb, wo, ffn_w, wqy, sk, *, seq, tm):
    t, d = xf.shape
    cw = u.shape[1]
    n_tiles = t // tm
    n_seq_tiles = seq // tm
    rb = tm // SUBLANES
    n_rb = t // SUBLANES
    row = lambda c: pl.BlockSpec((tm, c), lambda i: (i, 0))
    n_groups, n_keys, _ = sk.shape
    out_shapes = (jax.ShapeDtypeStruct((t, d), F32),
                  jax.ShapeDtypeStruct((d, t), BF16),
                  jax.ShapeDtypeStruct((n_groups, n_keys, t), F32))
    return pl.pallas_call(
        functools.partial(_merge_kernel, n_seq_tiles=n_seq_tiles),
        grid=(n_tiles,),
        in_specs=[row(attn.shape[1]), row(cw), row(cw),
                  pl.BlockSpec((SUBLANES, cw), lambda i: (jnp.maximum(i * rb - 1, 0), 0)),
                  pl.BlockSpec((SUBLANES, cw), lambda i: (jnp.minimum((i + 1) * rb, n_rb - 1), 0)),
                  row(d), row(d), row(d),
                  _const_spec(conv_w.shape), _const_spec(wpa.shape), _const_spec(wpb.shape),
                  _const_spec(wo.shape), _const_spec(ffn_w.shape), _const_spec(wqy.shape),
                  _const_spec(sk.shape)],
        out_specs=[row(d), pl.BlockSpec((d, tm), lambda i: (0, i)),
                   pl.BlockSpec((n_groups, n_keys, tm), lambda i: (0, 0, i))],
        out_shape=out_shapes,
        compiler_params=pltpu.CompilerParams(dimension_semantics=("arbitrary",),
                                             vmem_limit_bytes=VMEM_LIMIT_BYTES),
        name="merge",
    )(attn, bg, u, u, u, sga, sgb, xf, conv_w, wpa, wpb, wo, ffn_w, wqy, sk)


def _top16(s):
    n = s.shape[0]
    k = PEER_TOPK
    idx = lax.broadcasted_iota(jnp.int32, s.shape, 0)
    rank = jnp.full(s.shape, k, jnp.int32)
    cur = s
    vals = []
    for r in range(k):
        mx = jnp.max(cur, axis=0, keepdims=True)
        first = jnp.min(jnp.where(cur == mx, idx, n), axis=0, keepdims=True)
        hit = idx == first
        rank = jnp.where(hit, r, rank)
        cur = jnp.where(hit, NEG_INF, cur)
        vals.append(mx)
    return rank, vals, jnp.zeros((1, s.shape[1]), jnp.bool_)


def _batcher_sort_network(lo, hi):
    def merge(lo, hi, r):
        step = 2 * r
        if step < hi - lo:
            yield from merge(lo, hi, step)
            yield from merge(lo + r, hi, step)
            yield from ((i, i + r) for i in range(lo + r, hi - r, step))
        else:
            yield (lo, lo + r)
    if hi - lo >= 1:
        mid = lo + (hi - lo) // 2
        yield from _batcher_sort_network(lo, mid)
        yield from _batcher_sort_network(mid + 1, hi)
        yield from merge(lo, hi, 1)


def _bitonic_merge_network(n):
    d = n // 2
    while d >= 1:
        yield from ((i, i + d) for i in range(n) if not i & d)
        d //= 2


_SORT16 = tuple(_batcher_sort_network(0, PEER_TOPK - 1))
_MERGE16 = tuple(_bitonic_merge_network(PEER_TOPK))


def _sublane_pieces(s):
    return [s[i * SUBLANES:(i + 1) * SUBLANES] for i in range(s.shape[0] // SUBLANES)]


def _sorted_top16(pieces):
    k = PEER_TOPK
    n = len(pieces)
    assert k <= 2 * n and n <= k

    def exchange(rows, network):
        for i, j in network:
            rows[i], rows[j] = jnp.maximum(rows[i], rows[j]), jnp.minimum(rows[i], rows[j])

    rows = list(pieces)
    exchange(rows, [(i, j) for i, j in _SORT16 if j < n])
    shift = 1
    while shift < SUBLANES:
        other = [pltpu.roll(x, shift, axis=0) for x in rows]
        merged = []
        for i in range(k):
            mine = rows[i] if i < len(rows) else None
            theirs = other[k - 1 - i] if k - 1 - i < len(other) else None
            merged.append(mine if theirs is None else theirs if mine is None
                          else jnp.maximum(mine, theirs))
        rows = merged
        exchange(rows, _MERGE16)
        shift *= 2
    return rows


def _top16_network(pieces):
    k = PEER_TOPK
    assert len(pieces) == k
    rows = _sorted_top16(pieces)
    dup = rows[0] == rows[1]
    for r in range(1, k - 1):
        dup = jnp.logical_or(dup, rows[r] == rows[r + 1])
    n_ge = jnp.zeros(pieces[0].shape, F32)
    for p in pieces:
        n_ge = n_ge + jnp.where(p >= rows[k - 1], 1.0, 0.0)
    n_ge = jnp.sum(n_ge, axis=0, keepdims=True)
    return rows, jnp.logical_or(dup[0:1], n_ge != float(k))


def _rank_by_count(pieces, rows):
    out = []
    for p in pieces:
        rank = jnp.zeros(p.shape, F32)
        for r, v in enumerate(rows):
            rank = jnp.where(v > p, float(r + 1), rank)
        out.append(rank)
    return jnp.concatenate(out, axis=0)


def _route_kernel(sc_ref, c1_ref, r2_ref, a_ref, b_ref):
    tied = _route_pass(sc_ref, c1_ref, r2_ref, a_ref, b_ref, break_ties=False)

    @pl.when(jnp.max(tied.astype(F32)) > 0.0)
    def _():
        _route_pass(sc_ref, c1_ref, r2_ref, a_ref, b_ref, break_ties=True)


def _route_pass(sc_ref, c1_ref, r2_ref, a_ref, b_ref, *, break_ties):
    tl = sc_ref.shape[-1]
    k = PEER_TOPK
    i16 = lax.broadcasted_iota(jnp.int32, (k, tl), 0)
    i8 = lax.broadcasted_iota(jnp.int32, (SUBLANES, tl), 0)
    order = [i16] + [r1 * k + i8 for r1 in range(1, 8)] + [(8 + i8) * k]
    order = jnp.concatenate(order, axis=0)
    valid = [i16 >= 0] + [i8 < (k // (r1 + 1)) for r1 in range(1, 8)] + [i8 >= 0]
    valid = jnp.concatenate(valid, axis=0)
    big = 2 * k * k

    def head(h, tied):
        s1 = sc_ref[2 * h]
        s2 = sc_ref[2 * h + 1]
        if break_ties:
            rank1, v1, tied1 = _top16(s1)
            rank2, v2, tied2 = _top16(s2)
            rank2 = rank2.astype(F32)
        else:
            p1, p2 = _sublane_pieces(s1), _sublane_pieces(s2)
            rows1, tied1 = _top16_network(p1)
            rows2, tied2 = _top16_network(p2)
            v1 = [x[0:1] for x in rows1]
            v2 = [x[0:1] for x in rows2]
            rank2 = _rank_by_count(p2, rows2)
        v2a = jnp.concatenate(v2, axis=0)
        v1b = jnp.concatenate(v1[8:], axis=0)
        e1 = [jnp.exp(v - v1[0]) for v in v1]
        e2a = jnp.exp(v2a - v2[0])
        e1b = jnp.concatenate(e1[8:], axis=0)
        cand = ([v1[0] + v2a] + [v1[r1] + v2a[:SUBLANES] for r1 in range(1, 8)]
                + [v1b + v2[0]])
        cand = jnp.where(valid, jnp.concatenate(cand, axis=0), NEG_INF)
        prod = ([e1[0] * e2a] + [e1[r1] * e2a[:SUBLANES] for r1 in range(1, 8)]
                + [e1b * e2a[0:1]])
        prod = jnp.concatenate(prod, axis=0)
        if break_ties:
            cur = cand
            for _ in range(k):
                mx = jnp.max(cur, axis=0, keepdims=True)
                first = jnp.min(jnp.where(cur == mx, order, big), axis=0, keepdims=True)
                cur = jnp.where(order == first, NEG_INF, cur)
            taken = jnp.where(cur != cand, 1.0, 0.0)
        else:
            cand_pieces = _sublane_pieces(cand)
            kth = _sorted_top16(cand_pieces)[k - 1]
            taken = jnp.concatenate([jnp.where(c >= kth, 1.0, 0.0) for c in cand_pieces], axis=0)
        z = jnp.sum(taken * prod, axis=0, keepdims=True)
        counts = [jnp.sum(taken[0:k], axis=0, keepdims=True)]
        for r1 in range(1, 8):
            lo = k + (r1 - 1) * SUBLANES
            counts.append(jnp.sum(taken[lo:lo + SUBLANES], axis=0, keepdims=True))
        lo = k + 7 * SUBLANES
        counts += [taken[lo + j:lo + j + 1] for j in range(8)]
        if break_ties:
            c1 = jnp.zeros(s1.shape, F32)
            for r in range(k):
                c1 = jnp.where(rank1 == r, counts[r], c1)
        else:
            wide = [jnp.broadcast_to(c, (SUBLANES, tl)) for c in counts]
            c1 = []
            for p in p1:
                c = jnp.zeros(p.shape, F32)
                for r in range(k):
                    c = jnp.where(p == rows1[r], wide[r], c)
                c1.append(c)
            c1 = jnp.concatenate(c1, axis=0)
        c1_ref[h] = c1
        r2_ref[h] = rank2.astype(BF16)
        a_ref[h] = jnp.exp(s1 - v1[0]) / z
        b_ref[h] = jnp.exp(s2 - v2[0]).astype(BF16)
        n_taken = counts[0]
        for c in counts[1:]:
            n_taken = n_taken + c
        tied_now = jnp.logical_or(jnp.logical_or(tied1, tied2), n_taken != float(k))
        return jnp.maximum(tied, jnp.where(tied_now, 1.0, 0.0))

    return lax.fori_loop(0, sc_ref.shape[0] // 2, head, jnp.zeros((1, tl), F32),
                         unroll=1 if break_ties else 4)


def _route(sc, *, tl):
    n_groups, n_keys, t = sc.shape
    nh = n_groups // 2
    words = jax.ShapeDtypeStruct((nh, n_keys, t), F32)
    halfs = jax.ShapeDtypeStruct((nh, n_keys, t), BF16)
    spec = pl.BlockSpec((nh, n_keys, tl), lambda i: (0, 0, i))
    return pl.pallas_call(
        _route_kernel,
        grid=(t // tl,),
        in_specs=[pl.BlockSpec((n_groups, n_keys, tl), lambda i: (0, 0, i))],
        out_specs=[spec] * 4,
        out_shape=(words, halfs, words, halfs),
        compiler_params=pltpu.CompilerParams(dimension_semantics=("arbitrary",),
                                             vmem_limit_bytes=VMEM_LIMIT_BYTES),
        name="peer_route",
    )(sc)


def _bf16_rows(row, n_rows):
    packed_rows = 2 * SUBLANES
    tile = jnp.broadcast_to(row, (packed_rows, row.shape[1])).astype(BF16)
    return jnp.concatenate([tile] * (n_rows // packed_rows), axis=0)


def _expert_kernel(hn_ref, u_ref, vt_ref, c1_ref, a_ref, r2_ref, b_ref, h_ref, fw_ref,
                   o_ref, acc_ref, wt_ref, ht0_ref, ht1_ref, *, n_j, final_norm, lane_chunk):
    s = pl.program_id(0)
    n_heads, n_keys, tq = r2_ref.shape
    rows = c1_ref.shape[1]
    j_prev = jnp.maximum(s - 1, 0) % n_j

    @pl.when(s == 0)
    def _():
        ht1_ref[...] = jnp.zeros_like(ht1_ref)

    @pl.when(j_prev == 0)
    def _():
        acc_ref[...] = jnp.zeros_like(acc_ref)

    def step(ht_w_ref, ht_r_ref):
        for t0 in range(0, tq, lane_chunk):
            tsl = slice(t0, t0 + lane_chunk)
            ht_w_ref[:, tsl] = _dot(u_ref[...], hn_ref[:, tsl])
            for r in range(rows):
                x = ht_r_ref[r * n_keys:(r + 1) * n_keys, tsl]
                act = (x * (0.5 + 0.5 * lax.erf(x * (1.0 / math.sqrt(2.0))))).astype(BF16)
                g = None
                for hd in range(n_heads):
                    sel = r2_ref[hd, :, tsl] < _bf16_rows(c1_ref[hd, r:r + 1, tsl], n_keys)
                    term = (jnp.where(sel, b_ref[hd, :, tsl], jnp.zeros((), BF16))
                            * _bf16_rows(a_ref[hd, r:r + 1, tsl], n_keys))
                    g = term if g is None else g + term
                wt_ref[r * n_keys:(r + 1) * n_keys, tsl] = g * act
            acc_ref[:, tsl] += _dot(vt_ref[...], wt_ref[:, tsl])

    @pl.when(s % 2 == 0)
    def _():
        step(ht0_ref, ht1_ref)

    @pl.when(s % 2 == 1)
    def _():
        step(ht1_ref, ht0_ref)

    @pl.when(jnp.logical_and(j_prev == n_j - 1, s > 0))
    def _():
        h = h_ref[...] + acc_ref[...].T
        if final_norm:
            ms = jnp.mean(h * h, axis=-1, keepdims=True)
            h = h * lax.rsqrt(ms + NORM_EPS) * fw_ref[...]
        o_ref[...] = h


def _experts(hn_t, u_bf, vt_bf, c1, a, r2, b, h, final_w, *, tq, te, lane_chunk, final_norm):
    d, t = hn_t.shape
    n_experts = u_bf.shape[0]
    nh, n_keys, _ = r2.shape
    rows = te // n_keys
    n_j = n_experts // te
    n_tiles = (t // tq) * n_j
    cur = lambda s: jnp.minimum(s, n_tiles - 1)
    prev = lambda s: jnp.maximum(s - 1, 0)
    return pl.pallas_call(
        functools.partial(_expert_kernel, n_j=n_j, final_norm=final_norm, lane_chunk=lane_chunk),
        grid=(n_tiles + 1,),
        in_specs=[pl.BlockSpec((d, tq), lambda s: (0, cur(s) // n_j)),
                  pl.BlockSpec((te, d), lambda s: (cur(s) % n_j, 0)),
                  pl.BlockSpec((None, d, te), lambda s: (prev(s) % n_j, 0, 0)),
                  pl.BlockSpec((nh, rows, tq), lambda s: (0, prev(s) % n_j, prev(s) // n_j)),
                  pl.BlockSpec((nh, rows, tq), lambda s: (0, prev(s) % n_j, prev(s) // n_j)),
                  pl.BlockSpec((nh, n_keys, tq), lambda s: (0, 0, prev(s) // n_j)),
                  pl.BlockSpec((nh, n_keys, tq), lambda s: (0, 0, prev(s) // n_j)),
                  pl.BlockSpec((tq, d), lambda s: (prev(s) // n_j, 0)),
                  _const_spec(final_w.shape)],
        out_specs=pl.BlockSpec((tq, d), lambda s: (prev(s) // n_j, 0)),
        out_shape=jax.ShapeDtypeStruct((t, d), F32),
        scratch_shapes=[pltpu.VMEM((d, tq), F32), pltpu.VMEM((te, tq), BF16),
                        pltpu.VMEM((te, tq), F32), pltpu.VMEM((te, tq), F32)],
        compiler_params=pltpu.CompilerParams(dimension_semantics=("arbitrary",),
                                             vmem_limit_bytes=VMEM_LIMIT_BYTES),
        name="peer_experts",
    )(hn_t, u_bf, vt_bf, c1, a, r2, b, h, final_w)


def _rope_tables(seq, dim):
    inv_freq = 1.0 / (ROPE_THETA ** (jnp.arange(0, dim, 2, dtype=F32) / dim))
    ang = jnp.arange(seq, dtype=F32)[:, None] * inv_freq[None, :]
    ang = jnp.concatenate([ang, ang], axis=-1)
    return jnp.cos(ang), jnp.sin(ang)


def _tile(n, want):
    want = min(n, want)
    assert n % want == 0, (n, want)
    return want


def kernel(x, attn_norm_w, w_in, lambda_q1, lambda_k1, lambda_q2, lambda_k2, subln_w, conv_w,
           w_proj_attn, w_proj_conv, w_out, ffn_norm_w, w_query, sub_keys, expert_u, expert_v,
           final_norm_w):
    batch, seq, d = x.shape
    depth = w_in.shape[0]
    t = batch * seq
    head_dim = lambda_q1.shape[-1]
    ev = subln_w.shape[-1]
    attn_w = w_proj_attn.shape[1]
    conv_wd = conv_w.shape[-1]
    n_heads = attn_w // ev
    qk_w = n_heads * 2 * head_dim
    assert 2 * head_dim == LANES and ev == LANES
    assert w_in.shape[-1] == 2 * qk_w + attn_w + 3 * conv_wd + 2 * d
    peer_heads, _, n_keys, half_dim = sub_keys.shape[1:]
    assert n_keys == LANES and half_dim == LANES
    n_experts = expert_u.shape[1]

    cos, sin = _rope_tables(seq, head_dim)
    sin = jnp.concatenate([-sin[:, :head_dim // 2], sin[:, head_dim // 2:]], axis=-1)
    cos = jnp.concatenate([cos, cos], axis=-1)
    sin = jnp.concatenate([sin, sin], axis=-1)

    tm = _tile(seq, 256)
    tq_attn = _tile(seq, 1024)
    tk_attn = _tile(seq, 1024)
    tl = _tile(t, LANES)
    tq_peer = _tile(t, 512)
    te = _tile(n_experts, 2048)

    h = x.reshape(t, d)
    for l in range(depth):
        lambda_init = 0.8 - 0.6 * math.exp(-0.3 * l)
        offs = [0]
        for sz in (qk_w, qk_w, attn_w, conv_wd, conv_wd, conv_wd, d, d):
            offs.append(offs[-1] + sz)
        wl = w_in[l]
        wq, wk, wv, wb, wc, wx, wga, wgb = [wl[:, offs[n]:offs[n + 1]] for n in range(8)]
        ws = tuple(w.astype(BF16) for w in (wq, wk, wv, wb, wc, wx, wga, wgb))

        q, k, v, bg, u, sga, sgb = _inproj(
            h, attn_norm_w[l][None], cos, sin, ws, seq=seq, tm=tm,
            q_scale=1.0 / math.sqrt(head_dim), half=head_dim // 2)
        attn = _attention(
            q, k, v, lambda_q1[l][None], lambda_k1[l][None], lambda_q2[l][None],
            lambda_k2[l][None], subln_w[l][None], batch=batch, seq=seq, n_heads=n_heads,
            tq=tq_attn, tk=tk_attn, lambda_init=lambda_init)
        sk = sub_keys[l].reshape(peer_heads * 2, n_keys, half_dim).astype(BF16)
        h, hn, sc = _merge(
            attn, bg, u, sga, sgb, h, conv_w[l], w_proj_attn[l].astype(BF16),
            w_proj_conv[l].astype(BF16), w_out[l].astype(BF16), ffn_norm_w[l][None],
            w_query[l].astype(BF16), sk, seq=seq, tm=tm)
        c1, r2, a, b = _route(sc, tl=tl)
        vt_tiles = expert_v[l].astype(BF16).reshape(n_experts // te, te, d).transpose(0, 2, 1)
        h = _experts(
            hn, expert_u[l].astype(BF16), vt_tiles, c1, a, r2, b, h,
            final_norm_w[None], tq=tq_peer, te=te, lane_chunk=_tile(tq_peer, 256),
            final_norm=(l == depth - 1))
    return h.reshape(batch, seq, d)
```

```python
import functools
import math

import jax
import jax.numpy as jnp
from jax import lax
from jax.experimental import pallas as pl
from jax.experimental.pallas import tpu as pltpu

NORM_EPS = 1e-6
ROPE_THETA = 10000.0
PEER_TOPK = 16
LANES = 128
SUBLANES = 8
VMEM_LIMIT_BYTES = 56 * 1024 * 1024

F32 = jnp.float32
BF16 = jnp.bfloat16
NEG_INF = float("-inf")


def _nt_dot(a, b):
    return lax.dot_general(a, b, (((1,), (1,)), ((), ())), preferred_element_type=F32)


def _dot(a, b):
    return jnp.dot(a, b, preferred_element_type=F32)


def _const_spec(shape):
    nd = len(shape)
    return pl.BlockSpec(shape, lambda *_: (0,) * nd)


def _inproj_kernel(x_ref, nw_ref, cos_ref, sin_ref, wq_ref, wk_ref, wv_ref,
                   wb_ref, wc_ref, wx_ref, wga_ref, wgb_ref,
                   q_ref, k_ref, v_ref, bg_ref, u_ref, sga_ref, sgb_ref, *, q_scale, half):
    x = x_ref[...]
    ms = jnp.mean(x * x, axis=-1, keepdims=True)
    xn = (x * lax.rsqrt(ms + NORM_EPS) * nw_ref[...]).astype(BF16)
    cos = cos_ref[...]
    sin = sin_ref[...]
    lane = lax.broadcasted_iota(jnp.int32, cos.shape, 1)
    low = (lane % (2 * half)) < half

    def rope(w_ref, o_ref, scale):
        t = _dot(xn, w_ref[...])
        for blk in range(t.shape[1] // LANES):
            sl = slice(blk * LANES, (blk + 1) * LANES)
            tb = t[:, sl]
            partner = jnp.where(low, pltpu.roll(tb, LANES - half, axis=1),
                                pltpu.roll(tb, half, axis=1))
            r = tb * cos + partner * sin
            if scale != 1.0:
                r = r * scale
            o_ref[:, sl] = r.astype(o_ref.dtype)

    rope(wq_ref, q_ref, q_scale)
    rope(wk_ref, k_ref, 1.0)
    v_ref[...] = _dot(xn, wv_ref[...]).astype(v_ref.dtype)
    bg_ref[...] = _dot(xn, wb_ref[...])
    u_ref[...] = _dot(xn, wc_ref[...]) * _dot(xn, wx_ref[...])
    sga_ref[...] = jax.nn.sigmoid(_dot(xn, wga_ref[...]))
    sgb_ref[...] = jax.nn.sigmoid(_dot(xn, wgb_ref[...]))


def _inproj(xf, norm_w, cos, sin, ws, *, seq, tm, q_scale, half):
    t, d = xf.shape
    wq, wk, wv, wb, wc, wx, wga, wgb = ws
    n_seq_tiles = seq // tm
    row = lambda c: pl.BlockSpec((tm, c), lambda i: (i, 0))
    out_shapes = (
        jax.ShapeDtypeStruct((t, wq.shape[1]), BF16),
        jax.ShapeDtypeStruct((t, wk.shape[1]), BF16),
        jax.ShapeDtypeStruct((t, wv.shape[1]), BF16),
        jax.ShapeDtypeStruct((t, wb.shape[1]), F32),
        jax.ShapeDtypeStruct((t, wc.shape[1]), F32),
        jax.ShapeDtypeStruct((t, wga.shape[1]), F32),
        jax.ShapeDtypeStruct((t, wgb.shape[1]), F32),
    )
    return pl.pallas_call(
        functools.partial(_inproj_kernel, q_scale=q_scale, half=half),
        grid=(t // tm,),
        in_specs=[row(d), _const_spec((1, d)),
                  pl.BlockSpec((tm, LANES), lambda i: (i % n_seq_tiles, 0)),
                  pl.BlockSpec((tm, LANES), lambda i: (i % n_seq_tiles, 0))]
                 + [_const_spec(w.shape) for w in ws],
        out_specs=[row(s.shape[1]) for s in out_shapes],
        out_shape=out_shapes,
        compiler_params=pltpu.CompilerParams(dimension_semantics=("arbitrary",),
                                             vmem_limit_bytes=VMEM_LIMIT_BYTES),
        name="inproj",
    )(xf, norm_w, cos, sin, *ws)


def _attn_kernel(lq1_ref, lk1_ref, lq2_ref, lk2_ref, sw_ref, q_ref, k_ref, v_ref, o_ref,
                 *, tk, head_dim, lambda_init):
    tq = q_ref.shape[0]
    seq = k_ref.shape[0]
    q = q_ref[...]
    lane = lax.broadcasted_iota(jnp.int32, q.shape, 1)
    zero = jnp.zeros_like(q)
    q1 = jnp.where(lane < head_dim, q, zero)
    q2 = jnp.where(lane >= head_dim, q, zero)
    lam = (jnp.exp(jnp.sum(lq1_ref[...] * lk1_ref[...], axis=-1, keepdims=True))
           - jnp.exp(jnp.sum(lq2_ref[...] * lk2_ref[...], axis=-1, keepdims=True)) + lambda_init)

    def lane_fold(op, acc, s):
        for c in range(s.shape[1] // LANES):
            acc = op(acc, s[:, c * LANES:(c + 1) * LANES])
        return acc

    def softmax_values(qm):
        parts = []
        for j in range(seq // tk):
            s = _nt_dot(qm, k_ref[j * tk:(j + 1) * tk, :])
            mc = jnp.max(lane_fold(jnp.maximum, s[:, :LANES], s[:, LANES:]), axis=-1, keepdims=True)
            p = jnp.exp(s - mc)
            lsum = lane_fold(jnp.add, p[:, :LANES], p[:, LANES:])
            parts.append((mc, lsum, _dot(p.astype(BF16), v_ref[j * tk:(j + 1) * tk, :])))
        m = parts[0][0]
        for mc, _, _ in parts[1:]:
            m = jnp.maximum(m, mc)
        lsum = acc = None
        for mc, lc, oc in parts:
            w = jnp.exp(mc - m)
            lsum = w * lc if lsum is None else lsum + w * lc
            acc = w * oc if acc is None else acc + w * oc
        return jnp.sum(lsum, axis=-1, keepdims=True), acc

    l1, a1 = softmax_values(q1)
    l2, a2 = softmax_values(q2)
    o = a1 / l1 - lam * (a2 / l2)
    ms = jnp.mean(o * o, axis=-1, keepdims=True)
    o = o * lax.rsqrt(ms + NORM_EPS) * sw_ref[...]
    o_ref[...] = (o * (1.0 - lambda_init)).astype(o_ref.dtype)


def _attention(q, k, v, lq1, lk1, lq2, lk2, subln_w, *, batch, seq, n_heads, tq, tk, lambda_init):
    t = q.shape[0]
    ev = v.shape[1] // n_heads
    head_dim = lq1.shape[-1]
    nq = seq // tq
    small = lambda a: _const_spec(a.shape)
    return pl.pallas_call(
        functools.partial(_attn_kernel, tk=tk, head_dim=head_dim,
                          lambda_init=lambda_init),
        grid=(batch, n_heads, nq),
        in_specs=[small(lq1), small(lk1), small(lq2), small(lk2), small(subln_w),
                  pl.BlockSpec((tq, 2 * head_dim), lambda b, h, i: (b * nq + i, h)),
                  pl.BlockSpec((seq, 2 * head_dim), lambda b, h, i: (b, h)),
                  pl.BlockSpec((seq, ev), lambda b, h, i: (b, h))],
        out_specs=pl.BlockSpec((tq, ev), lambda b, h, i: (b * nq + i, h)),
        out_shape=jax.ShapeDtypeStruct((t, v.shape[1]), BF16),
        compiler_params=pltpu.CompilerParams(
            dimension_semantics=("arbitrary", "arbitrary", "arbitrary"),
            vmem_limit_bytes=VMEM_LIMIT_BYTES),
        name="diff_attn",
    )(lq1, lk1, lq2, lk2, subln_w, q, k, v)


def _merge_kernel(attn_ref, bg_ref, u_ref, up_ref, un_ref, sga_ref, sgb_ref, x_ref,
                  cw_ref, wpa_ref, wpb_ref, wo_ref, fw_ref, wqy_ref, sk_ref,
                  h_ref, hn_ref, sc_ref, *, n_seq_tiles):
    i = pl.program_id(0)
    tm = u_ref.shape[0]
    u = u_ref[...]
    row = lax.broadcasted_iota(jnp.int32, u.shape, 0)
    first = (i % n_seq_tiles) == 0
    last = (i % n_seq_tiles) == n_seq_tiles - 1
    prev_row = jnp.where(first, 0.0, up_ref[SUBLANES - 1:SUBLANES, :])
    next_row = jnp.where(last, 0.0, un_ref[0:1, :])
    u_prev = jnp.where(row == 0, prev_row, pltpu.roll(u, 1, axis=0))
    u_next = jnp.where(row == tm - 1, next_row, pltpu.roll(u, tm - 1, axis=0))
    cw = cw_ref[...]
    conv = cw[0:1, :] * u_prev + cw[1:2, :] * u + cw[2:3, :] * u_next
    y_conv = _dot((bg_ref[...] * conv).astype(BF16), wpb_ref[...])
    y_attn = _dot(attn_ref[...], wpa_ref[...])
    merged = sga_ref[...] * y_attn + sgb_ref[...] * y_conv
    h = x_ref[...] + _dot(merged.astype(BF16), wo_ref[...])
    h_ref[...] = h
    ms = jnp.mean(h * h, axis=-1, keepdims=True)
    hn_f32 = h * lax.rsqrt(ms + NORM_EPS) * fw_ref[...]
    hn = hn_f32.astype(BF16)
    hn_ref[...] = hn_f32.T.astype(BF16)
    qp = _dot(hn, wqy_ref[...]).astype(BF16)
    for g in range(sk_ref.shape[0]):
        sc_ref[g] = _nt_dot(sk_ref[g], qp[:, g * LANES:(g + 1) * LANES])


def _merge(attn, bg, u, sga, sgb, xf, conv_w, wpa, wpb, wo, ffn_w, wqy, sk, *, seq, tm):
    t, d = xf.shape
    cw = u.shape[1]
    n_tiles = t // tm
    n_seq_tiles = seq // tm
    rb = tm // SUBLANES
    n_rb = t // SUBLANES
    row = lambda c: pl.BlockSpec((tm, c), lambda i: (i, 0))
    n_groups, n_keys, _ = sk.shape
    out_shapes = (jax.ShapeDtypeStruct((t, d), F32),
                  jax.ShapeDtypeStruct((d, t), BF16),
                  jax.ShapeDtypeStruct((n_groups, n_keys, t), F32))
    return pl.pallas_call(
        functools.partial(_merge_kernel, n_seq_tiles=n_seq_tiles),
        grid=(n_tiles,),
        in_specs=[row(attn.shape[1]), row(cw), row(cw),
                  pl.BlockSpec((SUBLANES, cw), lambda i: (jnp.maximum(i * rb - 1, 0), 0)),
                  pl.BlockSpec((SUBLANES, cw), lambda i: (jnp.minimum((i + 1) * rb, n_rb - 1), 0)),
                  row(d), row(d), row(d),
                  _const_spec(conv_w.shape), _const_spec(wpa.shape), _const_spec(wpb.shape),
                  _const_spec(wo.shape), _const_spec(ffn_w.shape), _const_spec(wqy.shape),
                  _const_spec(sk.shape)],
        out_specs=[row(d), pl.BlockSpec((d, tm), lambda i: (0, i)),
                   pl.BlockSpec((n_groups, n_keys, tm), lambda i: (0, 0, i))],
        out_shape=out_shapes,
        compiler_params=pltpu.CompilerParams(dimension_semantics=("arbitrary",),
                                             vmem_limit_bytes=VMEM_LIMIT_BYTES),
        name="merge",
    )(attn, bg, u, u, u, sga, sgb, xf, conv_w, wpa, wpb, wo, ffn_w, wqy, sk)


def _top16(s):
    n = s.shape[0]
    k = PEER_TOPK
    idx = lax.broadcasted_iota(jnp.int32, s.shape, 0)
    rank = jnp.full(s.shape, k, jnp.int32)
    cur = s
    vals = []
    for r in range(k):
        mx = jnp.max(cur, axis=0, keepdims=True)
        first = jnp.min(jnp.where(cur == mx, idx, n), axis=0, keepdims=True)
        hit = idx == first
        rank = jnp.where(hit, r, rank)
        cur = jnp.where(hit, NEG_INF, cur)
        vals.append(mx)
    return rank, vals, jnp.zeros((1, s.shape[1]), jnp.bool_)


def _batcher_sort_network(lo, hi):
    def merge(lo, hi, r):
        step = 2 * r
        if step < hi - lo:
            yield from merge(lo, hi, step)
            yield from merge(lo + r, hi, step)
            yield from ((i, i + r) for i in range(lo + r, hi - r, step))
        else:
            yield (lo, lo + r)
    if hi - lo >= 1:
        mid = lo + (hi - lo) // 2
        yield from _batcher_sort_network(lo, mid)
        yield from _batcher_sort_network(mid + 1, hi)
        yield from merge(lo, hi, 1)


def _bitonic_merge_network(n):
    d = n // 2
    while d >= 1:
        yield from ((i, i + d) for i in range(n) if not i & d)
        d //= 2


_SORT16 = tuple(_batcher_sort_network(0, PEER_TOPK - 1))
_MERGE16 = tuple(_bitonic_merge_network(PEER_TOPK))


def _sublane_pieces(s):
    return [s[i * SUBLANES:(i + 1) * SUBLANES] for i in range(s.shape[0] // SUBLANES)]


def _sorted_top16(pieces):
    k = PEER_TOPK
    n = len(pieces)
    assert k <= 2 * n and n <= k

    def exchange(rows, network):
        for i, j in network:
            rows[i], rows[j] = jnp.maximum(rows[i], rows[j]), jnp.minimum(rows[i], rows[j])

    rows = list(pieces)
    exchange(rows, [(i, j) for i, j in _SORT16 if j < n])
    shift = 1
    while shift < SUBLANES:
        other = [pltpu.roll(x, shift, axis=0) for x in rows]
        merged = []
        for i in range(k):
            mine = rows[i] if i < len(rows) else None
            theirs = other[k - 1 - i] if k - 1 - i < len(other) else None
            merged.append(mine if theirs is None else theirs if mine is None
                          else jnp.maximum(mine, theirs))
        rows = merged
        exchange(rows, _MERGE16)
        shift *= 2
    return rows


def _top16_network(pieces):
    k = PEER_TOPK
    assert len(pieces) == k
    rows = _sorted_top16(pieces)
    dup = rows[0] == rows[1]
    for r in range(1, k - 1):
        dup = jnp.logical_or(dup, rows[r] == rows[r + 1])
    n_ge = jnp.zeros(pieces[0].shape, F32)
    for p in pieces:
        n_ge = n_ge + jnp.where(p >= rows[k - 1], 1.0, 0.0)
    n_ge = jnp.sum(n_ge, axis=0, keepdims=True)
    return rows, jnp.logical_or(dup[0:1], n_ge != float(k))


def _rank_by_count(pieces, rows):
    out = []
    for p in pieces:
        rank = jnp.zeros(p.shape, F32)
        for r, v in enumerate(rows):
            rank = jnp.where(v > p, float(r + 1), rank)
        out.append(rank)
    return jnp.concatenate(out, axis=0)


def _route_kernel(sc_ref, c1_ref, r2_ref, a_ref, b_ref):
    tied = _route_pass(sc_ref, c1_ref, r2_ref, a_ref, b_ref, break_ties=False)

    @pl.when(jnp.max(tied.astype(F32)) > 0.0)
    def _():
        _route_pass(sc_ref, c1_ref, r2_ref, a_ref, b_ref, break_ties=True)


def _route_pass(sc_ref, c1_ref, r2_ref, a_ref, b_ref, *, break_ties):
    tl = sc_ref.shape[-1]
    k = PEER_TOPK
    i16 = lax.broadcasted_iota(jnp.int32, (k, tl), 0)
    i8 = lax.broadcasted_iota(jnp.int32, (SUBLANES, tl), 0)
    order = [i16] + [r1 * k + i8 for r1 in range(1, 8)] + [(8 + i8) * k]
    order = jnp.concatenate(order, axis=0)
    valid = [i16 >= 0] + [i8 < (k // (r1 + 1)) for r1 in range(1, 8)] + [i8 >= 0]
    valid = jnp.concatenate(valid, axis=0)
    big = 2 * k * k

    def head(h, tied):
        s1 = sc_ref[2 * h]
        s2 = sc_ref[2 * h + 1]
        if break_ties:
            rank1, v1, tied1 = _top16(s1)
            rank2, v2, tied2 = _top16(s2)
            rank2 = rank2.astype(F32)
        else:
            p1, p2 = _sublane_pieces(s1), _sublane_pieces(s2)
            rows1, tied1 = _top16_network(p1)
            rows2, tied2 = _top16_network(p2)
            v1 = [x[0:1] for x in rows1]
            v2 = [x[0:1] for x in rows2]
            rank2 = _rank_by_count(p2, rows2)
        v2a = jnp.concatenate(v2, axis=0)
        v1b = jnp.concatenate(v1[8:], axis=0)
        e1 = [jnp.exp(v - v1[0]) for v in v1]
        e2a = jnp.exp(v2a - v2[0])
        e1b = jnp.concatenate(e1[8:], axis=0)
        cand = ([v1[0] + v2a] + [v1[r1] + v2a[:SUBLANES] for r1 in range(1, 8)]
                + [v1b + v2[0]])
        cand = jnp.where(valid, jnp.concatenate(cand, axis=0), NEG_INF)
        prod = ([e1[0] * e2a] + [e1[r1] * e2a[:SUBLANES] for r1 in range(1, 8)]
                + [e1b * e2a[0:1]])
        prod = jnp.concatenate(prod, axis=0)
        if break_ties:
            cur = cand
            for _ in range(k):
                mx = jnp.max(cur, axis=0, keepdims=True)
                first = jnp.min(jnp.where(cur == mx, order, big), axis=0, keepdims=True)
                cur = jnp.where(order == first, NEG_INF, cur)
            taken = jnp.where(cur != cand, 1.0, 0.0)
        else:
            cand_pieces = _sublane_pieces(cand)
            kth = _sorted_top16(cand_pieces)[k - 1]
            taken = jnp.concatenate([jnp.where(c >= kth, 1.0, 0.0) for c in cand_pieces], axis=0)
        z = jnp.sum(taken * prod, axis=0, keepdims=True)
        counts = [jnp.sum(taken[0:k], axis=0, keepdims=True)]
        for r1 in range(1, 8):
            lo = k + (r1 - 1) * SUBLANES
            counts.append(jnp.sum(taken[lo:lo + SUBLANES], axis=0, keepdims=True))
        lo = k + 7 * SUBLANES
        counts += [taken[lo + j:lo + j + 1] for j in range(8)]
        if break_ties:
            c1 = jnp.zeros(s1.shape, F32)
            for r in range(k):
                c1 = jnp.where(rank1 == r, counts[r], c1)
        else:
            wide = [jnp.broadcast_to(c, (SUBLANES, tl)) for c in counts]
            c1 = []
            for p in p1:
                c = jnp.zeros(p.shape, F32)
                for r in range(k):
                    c = jnp.where(p == rows1[r], wide[r], c)
                c1.append(c)
            c1 = jnp.concatenate(c1, axis=0)
        c1_ref[h] = c1
        r2_ref[h] = rank2.astype(BF16)
        a_ref[h] = jnp.exp(s1 - v1[0]) / z
        b_ref[h] = jnp.exp(s2 - v2[0]).astype(BF16)
        n_taken = counts[0]
        for c in counts[1:]:
            n_taken = n_taken + c
        tied_now = jnp.logical_or(jnp.logical_or(tied1, tied2), n_taken != float(k))
        return jnp.maximum(tied, jnp.where(tied_now, 1.0, 0.0))

    return lax.fori_loop(0, sc_ref.shape[0] // 2, head, jnp.zeros((1, tl), F32),
                         unroll=1 if break_ties else 4)


def _route(sc, *, tl):
    n_groups, n_keys, t = sc.shape
    nh = n_groups // 2
    words = jax.ShapeDtypeStruct((nh, n_keys, t), F32)
    halfs = jax.ShapeDtypeStruct((nh, n_keys, t), BF16)
    spec = pl.BlockSpec((nh, n_keys, tl), lambda i: (0, 0, i))
    return pl.pallas_call(
        _route_kernel,
        grid=(t // tl,),
        in_specs=[pl.BlockSpec((n_groups, n_keys, tl), lambda i: (0, 0, i))],
        out_specs=[spec] * 4,
        out_shape=(words, halfs, words, halfs),
        compiler_params=pltpu.CompilerParams(dimension_semantics=("arbitrary",),
                                             vmem_limit_bytes=VMEM_LIMIT_BYTES),
        name="peer_route",
    )(sc)


def _bf16_rows(row, n_rows):
    packed_rows = 2 * SUBLANES
    tile = jnp.broadcast_to(row, (packed_rows, row.shape[1])).astype(BF16)
    return jnp.concatenate([tile] * (n_rows // packed_rows), axis=0)


def _expert_kernel(hn_ref, u_ref, vt_ref, c1_ref, a_ref, r2_ref, b_ref, h_ref, fw_ref,
                   o_ref, acc_ref, wt_ref, ht0_ref, ht1_ref, *, n_j, final_norm, lane_chunk):
    s = pl.program_id(0)
    n_heads, n_keys, tq = r2_ref.shape
    rows = c1_ref.shape[1]
    j_prev = jnp.maximum(s - 1, 0) % n_j

    @pl.when(s == 0)
    def _():
        ht1_ref[...] = jnp.zeros_like(ht1_ref)

    @pl.when(j_prev == 0)
    def _():
        acc_ref[...] = jnp.zeros_like(acc_ref)

    def step(ht_w_ref, ht_r_ref):
        for t0 in range(0, tq, lane_chunk):
            tsl = slice(t0, t0 + lane_chunk)
            ht_w_ref[:, tsl] = _dot(u_ref[...], hn_ref[:, tsl])
            for r in range(rows):
                x = ht_r_ref[r * n_keys:(r + 1) * n_keys, tsl]
                act = (x * (0.5 + 0.5 * lax.erf(x * (1.0 / math.sqrt(2.0))))).astype(BF16)
                g = None
                for hd in range(n_heads):
                    sel = r2_ref[hd, :, tsl] < _bf16_rows(c1_ref[hd, r:r + 1, tsl], n_keys)
                    term = (jnp.where(sel, b_ref[hd, :, tsl], jnp.zeros((), BF16))
                            * _bf16_rows(a_ref[hd, r:r + 1, tsl], n_keys))
                    g = term if g is None else g + term
                wt_ref[r * n_keys:(r + 1) * n_keys, tsl] = g * act
            acc_ref[:, tsl] += _dot(vt_ref[...], wt_ref[:, tsl])

    @pl.when(s % 2 == 0)
    def _():
        step(ht0_ref, ht1_ref)

    @pl.when(s % 2 == 1)
    def _():
        step(ht1_ref, ht0_ref)

    @pl.when(jnp.logical_and(j_prev == n_j - 1, s > 0))
    def _():
        h = h_ref[...] + acc_ref[...].T
        if final_norm:
            ms = jnp.mean(h * h, axis=-1, keepdims=True)
            h = h * lax.rsqrt(ms + NORM_EPS) * fw_ref[...]
        o_ref[...] = h


def _experts(hn_t, u_bf, vt_bf, c1, a, r2, b, h, final_w, *, tq, te, lane_chunk, final_norm):
    d, t = hn_t.shape
    n_experts = u_bf.shape[0]
    nh, n_keys, _ = r2.shape
    rows = te // n_keys
    n_j = n_experts // te
    n_tiles = (t // tq) * n_j
    cur = lambda s: jnp.minimum(s, n_tiles - 1)
    prev = lambda s: jnp.maximum(s - 1, 0)
    return pl.pallas_call(
        functools.partial(_expert_kernel, n_j=n_j, final_norm=final_norm, lane_chunk=lane_chunk),
        grid=(n_tiles + 1,),
        in_specs=[pl.BlockSpec((d, tq), lambda s: (0, cur(s) // n_j)),
                  pl.BlockSpec((te, d), lambda s: (cur(s) % n_j, 0)),
                  pl.BlockSpec((None, d, te), lambda s: (prev(s) % n_j, 0, 0)),
                  pl.BlockSpec((nh, rows, tq), lambda s: (0, prev(s) % n_j, prev(s) // n_j)),
                  pl.BlockSpec((nh, rows, tq), lambda s: (0, prev(s) % n_j, prev(s) // n_j)),
                  pl.BlockSpec((nh, n_keys, tq), lambda s: (0, 0, prev(s) // n_j)),
                  pl.BlockSpec((nh, n_keys, tq), lambda s: (0, 0, prev(s) // n_j)),
                  pl.BlockSpec((tq, d), lambda s: (prev(s) // n_j, 0)),
                  _const_spec(final_w.shape)],
        out_specs=pl.BlockSpec((tq, d), lambda s: (prev(s) // n_j, 0)),
        out_shape=jax.ShapeDtypeStruct((t, d), F32),
        scratch_shapes=[pltpu.VMEM((d, tq), F32), pltpu.VMEM((te, tq), BF16),
                        pltpu.VMEM((te, tq), F32), pltpu.VMEM((te, tq), F32)],
        compiler_params=pltpu.CompilerParams(dimension_semantics=("arbitrary",),
                                             vmem_limit_bytes=VMEM_LIMIT_BYTES),
        name="peer_experts",
    )(hn_t, u_bf, vt_bf, c1, a, r2, b, h, final_w)


def _rope_tables(seq, dim):
    inv_freq = 1.0 / (ROPE_THETA ** (jnp.arange(0, dim, 2, dtype=F32) / dim))
    ang = jnp.arange(seq, dtype=F32)[:, None] * inv_freq[None, :]
    ang = jnp.concatenate([ang, ang], axis=-1)
    return jnp.cos(ang), jnp.sin(ang)


def _tile(n, want):
    want = min(n, want)
    assert n % want == 0, (n, want)
    return want


def kernel(x, attn_norm_w, w_in, lambda_q1, lambda_k1, lambda_q2, lambda_k2, subln_w, conv_w,
           w_proj_attn, w_proj_conv, w_out, ffn_norm_w, w_query, sub_keys, expert_u, expert_v,
           final_norm_w):
    batch, seq, d = x.shape
    depth = w_in.shape[0]
    t = batch * seq
    head_dim = lambda_q1.shape[-1]
    ev = subln_w.shape[-1]
    attn_w = w_proj_attn.shape[1]
    conv_wd = conv_w.shape[-1]
    n_heads = attn_w // ev
    qk_w = n_heads * 2 * head_dim
    assert 2 * head_dim == LANES and ev == LANES
    assert w_in.shape[-1] == 2 * qk_w + attn_w + 3 * conv_wd + 2 * d
    peer_heads, _, n_keys, half_dim = sub_keys.shape[1:]
    assert n_keys == LANES and half_dim == LANES
    n_experts = expert_u.shape[1]

    cos, sin = _rope_tables(seq, head_dim)
    sin = jnp.concatenate([-sin[:, :head_dim // 2], sin[:, head_dim // 2:]], axis=-1)
    cos = jnp.concatenate([cos, cos], axis=-1)
    sin = jnp.concatenate([sin, sin], axis=-1)

    tm = _tile(seq, 512)
    tq_attn = _tile(seq, 1024)
    tk_attn = _tile(seq, 1024)
    tl = _tile(t, LANES)
    tq_peer = _tile(t, 512)
    te = _tile(n_experts, 2048)

    h = x.reshape(t, d)
    for l in range(depth):
        lambda_init = 0.8 - 0.6 * math.exp(-0.3 * l)
        offs = [0]
        for sz in (qk_w, qk_w, attn_w, conv_wd, conv_wd, conv_wd, d, d):
            offs.append(offs[-1] + sz)
        wl = w_in[l]
        wq, wk, wv, wb, wc, wx, wga, wgb = [wl[:, offs[n]:offs[n + 1]] for n in range(8)]
        ws = tuple(w.astype(BF16) for w in (wq, wk, wv, wb, wc, wx, wga, wgb))

        q, k, v, bg, u, sga, sgb = _inproj(
            h, attn_norm_w[l][None], cos, sin, ws, seq=seq, tm=tm,
            q_scale=1.0 / math.sqrt(head_dim), half=head_dim // 2)
        attn = _attention(
            q, k, v, lambda_q1[l][None], lambda_k1[l][None], lambda_q2[l][None],
            lambda_k2[l][None], subln_w[l][None], batch=batch, seq=seq, n_heads=n_heads,
            tq=tq_attn, tk=tk_attn, lambda_init=lambda_init)
        sk = sub_keys[l].reshape(peer_heads * 2, n_keys, half_dim).astype(BF16)
        h, hn, sc = _merge(
            attn, bg, u, sga, sgb, h, conv_w[l], w_proj_attn[l].astype(BF16),
            w_proj_conv[l].astype(BF16), w_out[l].astype(BF16), ffn_norm_w[l][None],
            w_query[l].astype(BF16), sk, seq=seq, tm=tm)
        c1, r2, a, b = _route(sc, tl=tl)
        vt_tiles = expert_v[l].astype(BF16).reshape(n_experts // te, te, d).transpose(0, 2, 1)
        h = _experts(
            hn, expert_u[l].astype(BF16), vt_tiles, c1, a, r2, b, h,
            final_norm_w[None], tq=tq_peer, te=te, lane_chunk=_tile(tq_peer, 256),
            final_norm=(l == depth - 1))
    return h.reshape(batch, seq, d)
```

```python
import functools
import math

import jax
import jax.numpy as jnp
from jax import lax
from jax.experimental import pallas as pl
from jax.experimental.pallas import tpu as pltpu

NORM_EPS = 1e-6
ROPE_THETA = 10000.0
PEER_TOPK = 16
LANES = 128
SUBLANES = 8
VMEM_LIMIT_BYTES = 56 * 1024 * 1024

F32 = jnp.float32
BF16 = jnp.bfloat16
NEG_INF = float("-inf")


def _nt_dot(a, b):
    return lax.dot_general(a, b, (((1,), (1,)), ((), ())), preferred_element_type=F32)


def _dot(a, b):
    return jnp.dot(a, b, preferred_element_type=F32)


def _const_spec(shape):
    nd = len(shape)
    return pl.BlockSpec(shape, lambda *_: (0,) * nd)


def _inproj_kernel(x_ref, nw_ref, cos_ref, sin_ref, wq_ref, wk_ref, wv_ref,
                   wb_ref, wc_ref, wx_ref, wga_ref, wgb_ref,
                   q_ref, k_ref, v_ref, bg_ref, u_ref, sga_ref, sgb_ref, *, q_scale, half):
    x = x_ref[...]
    ms = jnp.mean(x * x, axis=-1, keepdims=True)
    xn = (x * lax.rsqrt(ms + NORM_EPS) * nw_ref[...]).astype(BF16)
    cos = cos_ref[...]
    sin = sin_ref[...]
    lane = lax.broadcasted_iota(jnp.int32, cos.shape, 1)
    low = (lane % (2 * half)) < half

    def rope(w_ref, o_ref, scale):
        t = _dot(xn, w_ref[...])
        for blk in range(t.shape[1] // LANES):
            sl = slice(blk * LANES, (blk + 1) * LANES)
            tb = t[:, sl]
            partner = jnp.where(low, pltpu.roll(tb, LANES - half, axis=1),
                                pltpu.roll(tb, half, axis=1))
            r = tb * cos + partner * sin
            if scale != 1.0:
                r = r * scale
            o_ref[:, sl] = r.astype(o_ref.dtype)

    rope(wq_ref, q_ref, q_scale)
    rope(wk_ref, k_ref, 1.0)
    v_ref[...] = _dot(xn, wv_ref[...]).astype(v_ref.dtype)
    bg_ref[...] = _dot(xn, wb_ref[...])
    u_ref[...] = _dot(xn, wc_ref[...]) * _dot(xn, wx_ref[...])
    sga_ref[...] = jax.nn.sigmoid(_dot(xn, wga_ref[...]))
    sgb_ref[...] = jax.nn.sigmoid(_dot(xn, wgb_ref[...]))


def _inproj(xf, norm_w, cos, sin, ws, *, seq, tm, q_scale, half):
    t, d = xf.shape
    wq, wk, wv, wb, wc, wx, wga, wgb = ws
    n_seq_tiles = seq // tm
    row = lambda c: pl.BlockSpec((tm, c), lambda i: (i, 0))
    out_shapes = (
        jax.ShapeDtypeStruct((t, wq.shape[1]), BF16),
        jax.ShapeDtypeStruct((t, wk.shape[1]), BF16),
        jax.ShapeDtypeStruct((t, wv.shape[1]), BF16),
        jax.ShapeDtypeStruct((t, wb.shape[1]), F32),
        jax.ShapeDtypeStruct((t, wc.shape[1]), F32),
        jax.ShapeDtypeStruct((t, wga.shape[1]), F32),
        jax.ShapeDtypeStruct((t, wgb.shape[1]), F32),
    )
    return pl.pallas_call(
        functools.partial(_inproj_kernel, q_scale=q_scale, half=half),
        grid=(t // tm,),
        in_specs=[row(d), _const_spec((1, d)),
                  pl.BlockSpec((tm, LANES), lambda i: (i % n_seq_tiles, 0)),
                  pl.BlockSpec((tm, LANES), lambda i: (i % n_seq_tiles, 0))]
                 + [_const_spec(w.shape) for w in ws],
        out_specs=[row(s.shape[1]) for s in out_shapes],
        out_shape=out_shapes,
        compiler_params=pltpu.CompilerParams(dimension_semantics=("arbitrary",),
                                             vmem_limit_bytes=VMEM_LIMIT_BYTES),
        name="inproj",
    )(xf, norm_w, cos, sin, *ws)


def _attn_kernel(lq1_ref, lk1_ref, lq2_ref, lk2_ref, sw_ref, q_ref, k_ref, v_ref, o_ref,
                 *, tk, head_dim, lambda_init):
    tq = q_ref.shape[0]
    seq = k_ref.shape[0]
    q = q_ref[...]
    lane = lax.broadcasted_iota(jnp.int32, q.shape, 1)
    zero = jnp.zeros_like(q)
    q1 = jnp.where(lane < head_dim, q, zero)
    q2 = jnp.where(lane >= head_dim, q, zero)
    lam = (jnp.exp(jnp.sum(lq1_ref[...] * lk1_ref[...], axis=-1, keepdims=True))
           - jnp.exp(jnp.sum(lq2_ref[...] * lk2_ref[...], axis=-1, keepdims=True)) + lambda_init)

    def lane_fold(op, acc, s):
        for c in range(s.shape[1] // LANES):
            acc = op(acc, s[:, c * LANES:(c + 1) * LANES])
        return acc

    def softmax_values(qm):
        parts = []
        for j in range(seq // tk):
            s = _nt_dot(qm, k_ref[j * tk:(j + 1) * tk, :])
            mc = jnp.max(lane_fold(jnp.maximum, s[:, :LANES], s[:, LANES:]), axis=-1, keepdims=True)
            p = jnp.exp(s - mc)
            lsum = lane_fold(jnp.add, p[:, :LANES], p[:, LANES:])
            parts.append((mc, lsum, _dot(p.astype(BF16), v_ref[j * tk:(j + 1) * tk, :])))
        m = parts[0][0]
        for mc, _, _ in parts[1:]:
            m = jnp.maximum(m, mc)
        lsum = acc = None
        for mc, lc, oc in parts:
            w = jnp.exp(mc - m)
            lsum = w * lc if lsum is None else lsum + w * lc
            acc = w * oc if acc is None else acc + w * oc
        return jnp.sum(lsum, axis=-1, keepdims=True), acc

    l1, a1 = softmax_values(q1)
    l2, a2 = softmax_values(q2)
    o = a1 / l1 - lam * (a2 / l2)
    ms = jnp.mean(o * o, axis=-1, keepdims=True)
    o = o * lax.rsqrt(ms + NORM_EPS) * sw_ref[...]
    o_ref[...] = (o * (1.0 - lambda_init)).astype(o_ref.dtype)


def _attention(q, k, v, lq1, lk1, lq2, lk2, subln_w, *, batch, seq, n_heads, tq, tk, lambda_init):
    t = q.shape[0]
    ev = v.shape[1] // n_heads
    head_dim = lq1.shape[-1]
    nq = seq // tq
    small = lambda a: _const_spec(a.shape)
    return pl.pallas_call(
        functools.partial(_attn_kernel, tk=tk, head_dim=head_dim,
                          lambda_init=lambda_init),
        grid=(batch, n_heads, nq),
        in_specs=[small(lq1), small(lk1), small(lq2), small(lk2), small(subln_w),
                  pl.BlockSpec((tq, 2 * head_dim), lambda b, h, i: (b * nq + i, h)),
                  pl.BlockSpec((seq, 2 * head_dim), lambda b, h, i: (b, h)),
                  pl.BlockSpec((seq, ev), lambda b, h, i: (b, h))],
        out_specs=pl.BlockSpec((tq, ev), lambda b, h, i: (b * nq + i, h)),
        out_shape=jax.ShapeDtypeStruct((t, v.shape[1]), BF16),
        compiler_params=pltpu.CompilerParams(
            dimension_semantics=("arbitrary", "arbitrary", "arbitrary"),
            vmem_limit_bytes=VMEM_LIMIT_BYTES),
        name="diff_attn",
    )(lq1, lk1, lq2, lk2, subln_w, q, k, v)


def _merge_kernel(attn_ref, bg_ref, u_ref, up_ref, un_ref, sga_ref, sgb_ref, x_ref,
                  cw_ref, wpa_ref, wpb_ref, wo_ref, fw_ref, wqy_ref, sk_ref,
                  h_ref, hn_ref, sc_ref, *, n_seq_tiles):
    i = pl.program_id(0)
    tm = u_ref.shape[0]
    u = u_ref[...]
    row = lax.broadcasted_iota(jnp.int32, u.shape, 0)
    first = (i % n_seq_tiles) == 0
    last = (i % n_seq_tiles) == n_seq_tiles - 1
    prev_row = jnp.where(first, 0.0, up_ref[SUBLANES - 1:SUBLANES, :])
    next_row = jnp.where(last, 0.0, un_ref[0:1, :])
    u_prev = jnp.where(row == 0, prev_row, pltpu.roll(u, 1, axis=0))
    u_next = jnp.where(row == tm - 1, next_row, pltpu.roll(u, tm - 1, axis=0))
    cw = cw_ref[...]
    conv = cw[0:1, :] * u_prev + cw[1:2, :] * u + cw[2:3, :] * u_next
    y_conv = _dot((bg_ref[...] * conv).astype(BF16), wpb_ref[...])
    y_attn = _dot(attn_ref[...], wpa_ref[...])
    merged = sga_ref[...] * y_attn + sgb_ref[...] * y_conv
    h = x_ref[...] + _dot(merged.astype(BF16), wo_ref[...])
    h_ref[...] = h
    ms = jnp.mean(h * h, axis=-1, keepdims=True)
    hn_f32 = h * lax.rsqrt(ms + NORM_EPS) * fw_ref[...]
    hn = hn_f32.astype(BF16)
    hn_ref[...] = hn_f32.T.astype(BF16)
    qp = _dot(hn, wqy_ref[...]).astype(BF16)
    for g in range(sk_ref.shape[0]):
        sc_ref[g] = _nt_dot(sk_ref[g], qp[:, g * LANES:(g + 1) * LANES])


def _merge(attn, bg, u, sga, sgb, xf, conv_w, wpa, wpb, wo, ffn_w, wqy, sk, *, seq, tm):
    t, d = xf.shape
    cw = u.shape[1]
    n_tiles = t // tm
    n_seq_tiles = seq // tm
    rb = tm // SUBLANES
    n_rb = t // SUBLANES
    row = lambda c: pl.BlockSpec((tm, c), lambda i: (i, 0))
    n_groups, n_keys, _ = sk.shape
    out_shapes = (jax.ShapeDtypeStruct((t, d), F32),
                  jax.ShapeDtypeStruct((d, t), BF16),
                  jax.ShapeDtypeStruct((n_groups, n_keys, t), F32))
    return pl.pallas_call(
        functools.partial(_merge_kernel, n_seq_tiles=n_seq_tiles),
        grid=(n_tiles,),
        in_specs=[row(attn.shape[1]), row(cw), row(cw),
                  pl.BlockSpec((SUBLANES, cw), lambda i: (jnp.maximum(i * rb - 1, 0), 0)),
                  pl.BlockSpec((SUBLANES, cw), lambda i: (jnp.minimum((i + 1) * rb, n_rb - 1), 0)),
                  row(d), row(d), row(d),
                  _const_spec(conv_w.shape), _const_spec(wpa.shape), _const_spec(wpb.shape),
                  _const_spec(wo.shape), _const_spec(ffn_w.shape), _const_spec(wqy.shape),
                  _const_spec(sk.shape)],
        out_specs=[row(d), pl.BlockSpec((d, tm), lambda i: (0, i)),
                   pl.BlockSpec((n_groups, n_keys, tm), lambda i: (0, 0, i))],
        out_shape=out_shapes,
        compiler_params=pltpu.CompilerParams(dimension_semantics=("arbitrary",),
                                             vmem_limit_bytes=VMEM_LIMIT_BYTES),
        name="merge",
    )(attn, bg, u, u, u, sga, sgb, xf, conv_w, wpa, wpb, wo, ffn_w, wqy, sk)


def _top16(s):
    n = s.shape[0]
    k = PEER_TOPK
    idx = lax.broadcasted_iota(jnp.int32, s.shape, 0)
    rank = jnp.full(s.shape, k, jnp.int32)
    cur = s
    vals = []
    for r in range(k):
        mx = jnp.max(cur, axis=0, keepdims=True)
        first = jnp.min(jnp.where(cur == mx, idx, n), axis=0, keepdims=True)
        hit = idx == first
        rank = jnp.where(hit, r, rank)
        cur = jnp.where(hit, NEG_INF, cur)
        vals.append(mx)
    return rank, vals, jnp.zeros((1, s.shape[1]), jnp.bool_)


def _batcher_sort_network(lo, hi):
    def merge(lo, hi, r):
        step = 2 * r
        if step < hi - lo:
            yield from merge(lo, hi, step)
            yield from merge(lo + r, hi, step)
            yield from ((i, i + r) for i in range(lo + r, hi - r, step))
        else:
            yield (lo, lo + r)
    if hi - lo >= 1:
        mid = lo + (hi - lo) // 2
        yield from _batcher_sort_network(lo, mid)
        yield from _batcher_sort_network(mid + 1, hi)
        yield from merge(lo, hi, 1)


def _bitonic_merge_network(n):
    d = n // 2
    while d >= 1:
        yield from ((i, i + d) for i in range(n) if not i & d)
        d //= 2


_SORT16 = tuple(_batcher_sort_network(0, PEER_TOPK - 1))
_MERGE16 = tuple(_bitonic_merge_network(PEER_TOPK))


def _sublane_pieces(s):
    return [s[i * SUBLANES:(i + 1) * SUBLANES] for i in range(s.shape[0] // SUBLANES)]


def _sorted_top16(pieces):
    k = PEER_TOPK
    n = len(pieces)
    assert k <= 2 * n and n <= k

    def exchange(rows, network):
        for i, j in network:
            rows[i], rows[j] = jnp.maximum(rows[i], rows[j]), jnp.minimum(rows[i], rows[j])

    rows = list(pieces)
    exchange(rows, [(i, j) for i, j in _SORT16 if j < n])
    shift = 1
    while shift < SUBLANES:
        other = [pltpu.roll(x, shift, axis=0) for x in rows]
        merged = []
        for i in range(k):
            mine = rows[i] if i < len(rows) else None
            theirs = other[k - 1 - i] if k - 1 - i < len(other) else None
            merged.append(mine if theirs is None else theirs if mine is None
                          else jnp.maximum(mine, theirs))
        rows = merged
        exchange(rows, _MERGE16)
        shift *= 2
    return rows


def _top16_network(pieces):
    k = PEER_TOPK
    assert len(pieces) == k
    rows = _sorted_top16(pieces)
    dup = rows[0] == rows[1]
    for r in range(1, k - 1):
        dup = jnp.logical_or(dup, rows[r] == rows[r + 1])
    n_ge = jnp.zeros(pieces[0].shape, F32)
    for p in pieces:
        n_ge = n_ge + jnp.where(p >= rows[k - 1], 1.0, 0.0)
    n_ge = jnp.sum(n_ge, axis=0, keepdims=True)
    return rows, jnp.logical_or(dup[0:1], n_ge != float(k))


def _rank_by_count(pieces, rows):
    out = []
    for p in pieces:
        rank = jnp.zeros(p.shape, F32)
        for r, v in enumerate(rows):
            rank = jnp.where(v > p, float(r + 1), rank)
        out.append(rank)
    return jnp.concatenate(out, axis=0)


def _route_kernel(sc_ref, c1_ref, r2_ref, a_ref, b_ref):
    tied = _route_pass(sc_ref, c1_ref, r2_ref, a_ref, b_ref, break_ties=False)

    @pl.when(jnp.max(tied.astype(F32)) > 0.0)
    def _():
        _route_pass(sc_ref, c1_ref, r2_ref, a_ref, b_ref, break_ties=True)


def _route_pass(sc_ref, c1_ref, r2_ref, a_ref, b_ref, *, break_ties):
    tl = sc_ref.shape[-1]
    k = PEER_TOPK
    i16 = lax.broadcasted_iota(jnp.int32, (k, tl), 0)
    i8 = lax.broadcasted_iota(jnp.int32, (SUBLANES, tl), 0)
    order = [i16] + [r1 * k + i8 for r1 in range(1, 8)] + [(8 + i8) * k]
    order = jnp.concatenate(order, axis=0)
    valid = [i16 >= 0] + [i8 < (k // (r1 + 1)) for r1 in range(1, 8)] + [i8 >= 0]
    valid = jnp.concatenate(valid, axis=0)
    big = 2 * k * k

    def head(h, tied):
        s1 = sc_ref[2 * h]
        s2 = sc_ref[2 * h + 1]
        if break_ties:
            rank1, v1, tied1 = _top16(s1)
            rank2, v2, tied2 = _top16(s2)
            rank2 = rank2.astype(F32)
        else:
            p1, p2 = _sublane_pieces(s1), _sublane_pieces(s2)
            rows1, tied1 = _top16_network(p1)
            rows2, tied2 = _top16_network(p2)
            v1 = [x[0:1] for x in rows1]
            v2 = [x[0:1] for x in rows2]
            rank2 = _rank_by_count(p2, rows2)
        v2a = jnp.concatenate(v2, axis=0)
        v1b = jnp.concatenate(v1[8:], axis=0)
        e1 = [jnp.exp(v - v1[0]) for v in v1]
        e2a = jnp.exp(v2a - v2[0])
        e1b = jnp.concatenate(e1[8:], axis=0)
        cand = ([v1[0] + v2a] + [v1[r1] + v2a[:SUBLANES] for r1 in range(1, 8)]
                + [v1b + v2[0]])
        cand = jnp.where(valid, jnp.concatenate(cand, axis=0), NEG_INF)
        prod = ([e1[0] * e2a] + [e1[r1] * e2a[:SUBLANES] for r1 in range(1, 8)]
                + [e1b * e2a[0:1]])
        prod = jnp.concatenate(prod, axis=0)
        if break_ties:
            cur = cand
            for _ in range(k):
                mx = jnp.max(cur, axis=0, keepdims=True)
                first = jnp.min(jnp.where(cur == mx, order, big), axis=0, keepdims=True)
                cur = jnp.where(order == first, NEG_INF, cur)
            taken = jnp.where(cur != cand, 1.0, 0.0)
        else:
            cand_pieces = _sublane_pieces(cand)
            kth = _sorted_top16(cand_pieces)[k - 1]
            taken = jnp.concatenate([jnp.where(c >= kth, 1.0, 0.0) for c in cand_pieces], axis=0)
        z = jnp.sum(taken * prod, axis=0, keepdims=True)
        counts = [jnp.sum(taken[0:k], axis=0, keepdims=True)]
        for r1 in range(1, 8):
            lo = k + (r1 - 1) * SUBLANES
            counts.append(jnp.sum(taken[lo:lo + SUBLANES], axis=0, keepdims=True))
        lo = k + 7 * SUBLANES
        counts += [taken[lo + j:lo + j + 1] for j in range(8)]
        if break_ties:
            c1 = jnp.zeros(s1.shape, F32)
            for r in range(k):
                c1 = jnp.where(rank1 == r, counts[r], c1)
        else:
            wide = [jnp.broadcast_to(c, (SUBLANES, tl)) for c in counts]
            c1 = []
            for p in p1:
                c = jnp.zeros(p.shape, F32)
                for r in range(k):
                    c = jnp.where(p == rows1[r], wide[r], c)
                c1.append(c)
            c1 = jnp.concatenate(c1, axis=0)
        c1_ref[h] = c1
        r2_ref[h] = rank2.astype(BF16)
        a_ref[h] = jnp.exp(s1 - v1[0]) / z
        b_ref[h] = jnp.exp(s2 - v2[0]).astype(BF16)
        n_taken = counts[0]
        for c in counts[1:]:
            n_taken = n_taken + c
        tied_now = jnp.logical_or(jnp.logical_or(tied1, tied2), n_taken != float(k))
        return jnp.maximum(tied, jnp.where(tied_now, 1.0, 0.0))

    return lax.fori_loop(0, sc_ref.shape[0] // 2, head, jnp.zeros((1, tl), F32),
                         unroll=1 if break_ties else 4)


def _route(sc, *, tl):
    n_groups, n_keys, t = sc.shape
    nh = n_groups // 2
    words = jax.ShapeDtypeStruct((nh, n_keys, t), F32)
    halfs = jax.ShapeDtypeStruct((nh, n_keys, t), BF16)
    spec = pl.BlockSpec((nh, n_keys, tl), lambda i: (0, 0, i))
    return pl.pallas_call(
        _route_kernel,
        grid=(t // tl,),
        in_specs=[pl.BlockSpec((n_groups, n_keys, tl), lambda i: (0, 0, i))],
        out_specs=[spec] * 4,
        out_shape=(words, halfs, words, halfs),
        compiler_params=pltpu.CompilerParams(dimension_semantics=("arbitrary",),
                                             vmem_limit_bytes=VMEM_LIMIT_BYTES),
        name="peer_route",
    )(sc)


def _bf16_rows(row, n_rows):
    packed_rows = 2 * SUBLANES
    tile = jnp.broadcast_to(row, (packed_rows, row.shape[1])).astype(BF16)
    return jnp.concatenate([tile] * (n_rows // packed_rows), axis=0)


def _expert_kernel(hn_ref, u_ref, vt_ref, c1_ref, a_ref, r2_ref, b_ref, h_ref, fw_ref,
                   o_ref, acc_ref, wt_ref, ht0_ref, ht1_ref, *, n_j, final_norm, lane_chunk):
    s = pl.program_id(0)
    n_heads, n_keys, tq = r2_ref.shape
    rows = c1_ref.shape[1]
    j_prev = jnp.maximum(s - 1, 0) % n_j

    @pl.when(s == 0)
    def _():
        ht1_ref[...] = jnp.zeros_like(ht1_ref)

    @pl.when(j_prev == 0)
    def _():
        acc_ref[...] = jnp.zeros_like(acc_ref)

    def step(ht_w_ref, ht_r_ref):
        for t0 in range(0, tq, lane_chunk):
            tsl = slice(t0, t0 + lane_chunk)
            ht_w_ref[:, tsl] = _dot(u_ref[...], hn_ref[:, tsl])
            for r in range(rows):
                x = ht_r_ref[r * n_keys:(r + 1) * n_keys, tsl]
                act = (x * (0.5 + 0.5 * lax.erf(x * (1.0 / math.sqrt(2.0))))).astype(BF16)
                g = None
                for hd in range(n_heads):
                    sel = r2_ref[hd, :, tsl] < _bf16_rows(c1_ref[hd, r:r + 1, tsl], n_keys)
                    term = (jnp.where(sel, b_ref[hd, :, tsl], jnp.zeros((), BF16))
                            * _bf16_rows(a_ref[hd, r:r + 1, tsl], n_keys))
                    g = term if g is None else g + term
                wt_ref[r * n_keys:(r + 1) * n_keys, tsl] = g * act
            acc_ref[:, tsl] += _dot(vt_ref[...], wt_ref[:, tsl])

    @pl.when(s % 2 == 0)
    def _():
        step(ht0_ref, ht1_ref)

    @pl.when(s % 2 == 1)
    def _():
        step(ht1_ref, ht0_ref)

    @pl.when(jnp.logical_and(j_prev == n_j - 1, s > 0))
    def _():
        h = h_ref[...] + acc_ref[...].T
        if final_norm:
            ms = jnp.mean(h * h, axis=-1, keepdims=True)
            h = h * lax.rsqrt(ms + NORM_EPS) * fw_ref[...]
        o_ref[...] = h


def _experts(hn_t, u_bf, vt_bf, c1, a, r2, b, h, final_w, *, tq, te, lane_chunk, final_norm):
    d, t = hn_t.shape
    n_experts = u_bf.shape[0]
    nh, n_keys, _ = r2.shape
    rows = te // n_keys
    n_j = n_experts // te
    n_tiles = (t // tq) * n_j
    cur = lambda s: jnp.minimum(s, n_tiles - 1)
    prev = lambda s: jnp.maximum(s - 1, 0)
    return pl.pallas_call(
        functools.partial(_expert_kernel, n_j=n_j, final_norm=final_norm, lane_chunk=lane_chunk),
        grid=(n_tiles + 1,),
        in_specs=[pl.BlockSpec((d, tq), lambda s: (0, cur(s) // n_j)),
                  pl.BlockSpec((te, d), lambda s: (cur(s) % n_j, 0)),
                  pl.BlockSpec((None, d, te), lambda s: (prev(s) % n_j, 0, 0)),
                  pl.BlockSpec((nh, rows, tq), lambda s: (0, prev(s) % n_j, prev(s) // n_j)),
                  pl.BlockSpec((nh, rows, tq), lambda s: (0, prev(s) % n_j, prev(s) // n_j)),
                  pl.BlockSpec((nh, n_keys, tq), lambda s: (0, 0, prev(s) // n_j)),
                  pl.BlockSpec((nh, n_keys, tq), lambda s: (0, 0, prev(s) // n_j)),
                  pl.BlockSpec((tq, d), lambda s: (prev(s) // n_j, 0)),
                  _const_spec(final_w.shape)],
        out_specs=pl.BlockSpec((tq, d), lambda s: (prev(s) // n_j, 0)),
        out_shape=jax.ShapeDtypeStruct((t, d), F32),
        scratch_shapes=[pltpu.VMEM((d, tq), F32), pltpu.VMEM((te, tq), BF16),
                        pltpu.VMEM((te, tq), F32), pltpu.VMEM((te, tq), F32)],
        compiler_params=pltpu.CompilerParams(dimension_semantics=("arbitrary",),
                                             vmem_limit_bytes=VMEM_LIMIT_BYTES),
        name="peer_experts",
    )(hn_t, u_bf, vt_bf, c1, a, r2, b, h, final_w)


def _rope_tables(seq, dim):
    inv_freq = 1.0 / (ROPE_THETA ** (jnp.arange(0, dim, 2, dtype=F32) / dim))
    ang = jnp.arange(seq, dtype=F32)[:, None] * inv_freq[None, :]
    ang = jnp.concatenate([ang, ang], axis=-1)
    return jnp.cos(ang), jnp.sin(ang)


def _tile(n, want):
    want = min(n, want)
    assert n % want == 0, (n, want)
    return want


def kernel(x, attn_norm_w, w_in, lambda_q1, lambda_k1, lambda_q2, lambda_k2, subln_w, conv_w,
           w_proj_attn, w_proj_conv, w_out, ffn_norm_w, w_query, sub_keys, expert_u, expert_v,
           final_norm_w):
    batch, seq, d = x.shape
    depth = w_in.shape[0]
    t = batch * seq
    head_dim = lambda_q1.shape[-1]
    ev = subln_w.shape[-1]
    attn_w = w_proj_attn.shape[1]
    conv_wd = conv_w.shape[-1]
    n_heads = attn_w // ev
    qk_w = n_heads * 2 * head_dim
    assert 2 * head_dim == LANES and ev == LANES
    assert w_in.shape[-1] == 2 * qk_w + attn_w + 3 * conv_wd + 2 * d
    peer_heads, _, n_keys, half_dim = sub_keys.shape[1:]
    assert n_keys == LANES and half_dim == LANES
    n_experts = expert_u.shape[1]

    cos, sin = _rope_tables(seq, head_dim)
    sin = jnp.concatenate([-sin[:, :head_dim // 2], sin[:, head_dim // 2:]], axis=-1)
    cos = jnp.concatenate([cos, cos], axis=-1)
    sin = jnp.concatenate([sin, sin], axis=-1)

    tm = _tile(seq, 512)
    tq_attn = _tile(seq, 1024)
    tk_attn = _tile(seq, 1024)
    tl = _tile(t, 2 * LANES)
    tq_peer = _tile(t, 512)
    te = _tile(n_experts, 2048)

    h = x.reshape(t, d)
    for l in range(depth):
        lambda_init = 0.8 - 0.6 * math.exp(-0.3 * l)
        offs = [0]
        for sz in (qk_w, qk_w, attn_w, conv_wd, conv_wd, conv_wd, d, d):
            offs.append(offs[-1] + sz)
        wl = w_in[l]
        wq, wk, wv, wb, wc, wx, wga, wgb = [wl[:, offs[n]:offs[n + 1]] for n in range(8)]
        ws = tuple(w.astype(BF16) for w in (wq, wk, wv, wb, wc, wx, wga, wgb))

        q, k, v, bg, u, sga, sgb = _inproj(
            h, attn_norm_w[l][None], cos, sin, ws, seq=seq, tm=tm,
            q_scale=1.0 / math.sqrt(head_dim), half=head_dim // 2)
        attn = _attention(
            q, k, v, lambda_q1[l][None], lambda_k1[l][None], lambda_q2[l][None],
            lambda_k2[l][None], subln_w[l][None], batch=batch, seq=seq, n_heads=n_heads,
            tq=tq_attn, tk=tk_attn, lambda_init=lambda_init)
        sk = sub_keys[l].reshape(peer_heads * 2, n_keys, half_dim).astype(BF16)
        h, hn, sc = _merge(
            attn, bg, u, sga, sgb, h, conv_w[l], w_proj_attn[l].astype(BF16),
            w_proj_conv[l].astype(BF16), w_out[l].astype(BF16), ffn_norm_w[l][None],
            w_query[l].astype(BF16), sk, seq=seq, tm=tm)
        c1, r2, a, b = _route(sc, tl=tl)
        vt_tiles = expert_v[l].astype(BF16).reshape(n_experts // te, te, d).transpose(0, 2, 1)
        h = _experts(
            hn, expert_u[l].astype(BF16), vt_tiles, c1, a, r2, b, h,
            final_norm_w[None], tq=tq_peer, te=te, lane_chunk=_tile(tq_peer, 256),
            final_norm=(l == depth - 1))
    return h.reshape(batch, seq, d)
```
